```python
import math
import jax, jax.numpy as jnp
from jax import lax
import numpy as np

D_MODEL = 1024
BATCH = 16
SEQ = 2048
DEPTH = 4

GRID_W = 64
CTX_LEN = 256
N_EVEN = (DEPTH + 1) // 2
N_ODD = DEPTH // 2
N_MOD = 6
A_HEAD_DIM = 64
A_HEADS = D_MODEL // A_HEAD_DIM
A_KV_HEADS = A_HEADS // 4
A_GROUP = A_HEADS // A_KV_HEADS
WINDOW = 128
BLOCK = 128
B_HEAD_DIM = 64
B_V_DIM = 2 * B_HEAD_DIM
B_HEADS = D_MODEL // B_V_DIM
D_FF = 2816
N_EXPERTS = 8
TOP_K = 2
D_FF_EXPERT = 3584
ROPE_THETA = 10000.0
LN_EPS = 1e-5
RMS_EPS = 1e-5
NEG_INF = -1e30
DEEPNORM_ALPHA = (2.0 * DEPTH) ** 0.25
DEEPNORM_BETA = (8.0 * DEPTH) ** -0.25
MOD_INIT_SCALE = 0.2

kernel_name = 'hybrid_interleaved_swa_diffattn_moe_dit'


def layer_norm(x, g, b):
    xf = x.astype(jnp.float32)
    mu = jnp.mean(xf, axis=-1, keepdims=True)
    var = jnp.mean(jnp.square(xf - mu), axis=-1, keepdims=True)
    y = (xf - mu) * lax.rsqrt(var + LN_EPS)
    return (y * g.astype(jnp.float32) + b.astype(jnp.float32)).astype(x.dtype)


def rms_norm(x, g):
    xf = x.astype(jnp.float32)
    y = xf * lax.rsqrt(jnp.mean(xf * xf, axis=-1, keepdims=True) + RMS_EPS)
    return (y * g.astype(jnp.float32)).astype(x.dtype)


def axial_rope_tables(n_tokens, head_dim, dtype):
    rows = n_tokens // GRID_W
    row = jnp.repeat(jnp.arange(rows, dtype=jnp.float32), GRID_W)
    col = jnp.tile(jnp.arange(GRID_W, dtype=jnp.float32), rows)
    half = head_dim // 2
    inv = ROPE_THETA ** (-jnp.arange(0, half, 2, dtype=jnp.float32) / half)
    ang_r = row[:, None] * inv[None, :]
    ang_c = col[:, None] * inv[None, :]
    ang = jnp.concatenate([ang_r, ang_r, ang_c, ang_c], axis=-1)
    return jnp.cos(ang).astype(dtype), jnp.sin(ang).astype(dtype)


def apply_axial_rope(x, cos, sin):
    x1, x2, x3, x4 = jnp.split(x, 4, axis=-1)
    rot = jnp.concatenate([-x2, x1, -x4, x3], axis=-1)
    return x * cos[None, :, None, :] + rot * sin[None, :, None, :]


def modulation(cond, w_mod, b_mod):
    m = jnp.dot(jax.nn.silu(cond), w_mod) + b_mod
    return jnp.split(m, N_MOD, axis=-1)


def split_heads(t, n_heads, head_dim):
    return t.reshape(t.shape[0], t.shape[1], n_heads, head_dim)


def windowed_gqa(hl, hc, w_qkv, sink, w_o, cos, sin, with_ctx):
    B, L, _ = hl.shape
    qd = A_HEADS * A_HEAD_DIM
    kd = A_KV_HEADS * A_HEAD_DIM

    def proj(h):
        q, k, v = jnp.split(h @ w_qkv, [qd, qd + kd], axis=-1)
        return (split_heads(q, A_HEADS, A_HEAD_DIM), split_heads(k, A_KV_HEADS, A_HEAD_DIM),
                split_heads(v, A_KV_HEADS, A_HEAD_DIM))

    ql, kl, vl = proj(hl)
    qc, kc, vc = proj(hc)
    ql = apply_axial_rope(ql, cos, sin)
    kl = apply_axial_rope(kl, cos, sin)
    scale = A_HEAD_DIM ** -0.5
    nb = L // BLOCK

    qb = ql.reshape(B, nb, BLOCK, A_KV_HEADS, A_GROUP, A_HEAD_DIM)
    pad = ((0, 0), (BLOCK, BLOCK), (0, 0), (0, 0))
    kp = jnp.pad(kl, pad).reshape(B, nb + 2, BLOCK, A_KV_HEADS, A_HEAD_DIM)
    vp = jnp.pad(vl, pad).reshape(B, nb + 2, BLOCK, A_KV_HEADS, A_HEAD_DIM)
    kband = jnp.concatenate([kp[:, :-2], kp[:, 1:-1], kp[:, 2:]], axis=2)
    vband = jnp.concatenate([vp[:, :-2], vp[:, 1:-1], vp[:, 2:]], axis=2)

    qpos = jnp.arange(L).reshape(nb, BLOCK)
    kpos = (jnp.arange(nb)[:, None] - 1) * BLOCK + jnp.arange(3 * BLOCK)[None, :]
    rel = kpos[:, None, :] - qpos[:, :, None]
    band_mask = (jnp.abs(rel) <= WINDOW) & (kpos[:, None, :] >= 0) & (kpos[:, None, :] < L)

    s_band = jnp.einsum('bnqhgd,bnkhd->bnhgqk', qb, kband).astype(jnp.float32) * scale
    s_band = jnp.where(band_mask[None, :, None, None], s_band, NEG_INF)
    s_ctx = jnp.einsum('bnqhgd,bchd->bnhgqc', qb, kc).astype(jnp.float32) * scale
    sink_g = sink.astype(jnp.float32).reshape(A_KV_HEADS, A_GROUP, 1, 1)
    s_sink = jnp.broadcast_to(sink_g, s_band.shape[:-1] + (1,))
    p = jax.nn.softmax(jnp.concatenate([s_band, s_ctx, s_sink], axis=-1), axis=-1)
    p_band = p[..., :3 * BLOCK].astype(vl.dtype)
    p_ctx = p[..., 3 * BLOCK:3 * BLOCK + kc.shape[1]].astype(vl.dtype)
    ol = (jnp.einsum('bnhgqk,bnkhd->bnqhgd', p_band, vband)
          + jnp.einsum('bnhgqc,bchd->bnqhgd', p_ctx, vc))
    ol = ol.reshape(B, L, qd) @ w_o

    oc = None
    if with_ctx:
        C = hc.shape[1]
        qcg = qc.reshape(B, C, A_KV_HEADS, A_GROUP, A_HEAD_DIM)
        s = jnp.einsum('bqhgd,bkhd->bhgqk', qcg, kc).astype(jnp.float32) * scale
        s_sink_c = jnp.broadcast_to(sink_g, s.shape[:-1] + (1,))
        pc = jax.nn.softmax(jnp.concatenate([s, s_sink_c], axis=-1), axis=-1)
        oc = jnp.einsum('bhgqk,bkhd->bqhgd', pc[..., :C].astype(vc.dtype), vc)
        oc = oc.reshape(B, C, qd) @ w_o
    return ol, oc


def diff_core(q, k, v, lam, subln_g, lam_init):
    n = q.shape[1]
    s = jnp.einsum('bqhd,bkhd->bhqk', q, k).astype(jnp.float32) * (B_HEAD_DIM ** -0.5)
    p = jax.nn.softmax(s, axis=-1).reshape(q.shape[0], B_HEADS, 2, n, k.shape[1])
    a = (p[:, :, 0] - lam * p[:, :, 1]).astype(v.dtype)
    o = jnp.einsum('bhqk,bkhd->bqhd', a, v)
    return rms_norm(o, subln_g) * (1.0 - lam_init)


def diff_attention(hl, hc, w_qkv, lam_q1, lam_k1, lam_q2, lam_k2, subln_g, w_o, cos, sin, lam_init, with_ctx):
    B, L, _ = hl.shape
    qd = 2 * B_HEADS * B_HEAD_DIM

    def proj(h):
        q, k, v = jnp.split(h @ w_qkv, [qd, 2 * qd], axis=-1)
        return (split_heads(q, 2 * B_HEADS, B_HEAD_DIM), split_heads(k, 2 * B_HEADS, B_HEAD_DIM),
                split_heads(v, B_HEADS, B_V_DIM))

    ql, kl, vl = proj(hl)
    qc, kc, vc = proj(hc)
    ql = apply_axial_rope(ql, cos, sin)
    kl = apply_axial_rope(kl, cos, sin)
    lam = (jnp.exp(jnp.sum(lam_q1.astype(jnp.float32) * lam_k1.astype(jnp.float32)))
           - jnp.exp(jnp.sum(lam_q2.astype(jnp.float32) * lam_k2.astype(jnp.float32))) + lam_init)

    k_all = jnp.concatenate([kl, kc], axis=1)
    v_all = jnp.concatenate([vl, vc], axis=1)
    nb = L // BLOCK
    qb = ql.reshape(B, nb, BLOCK, 2 * B_HEADS, B_HEAD_DIM).swapaxes(0, 1)
    ob = lax.map(lambda q: diff_core(q, k_all, v_all, lam, subln_g, lam_init), qb)
    ol = ob.swapaxes(0, 1).reshape(B, L, B_HEADS * B_V_DIM) @ w_o

    oc = None
    if with_ctx:
        oc = diff_core(qc, kc, vc, lam, subln_g, lam_init)
        oc = oc.reshape(B, hc.shape[1], B_HEADS * B_V_DIM) @ w_o
    return ol, oc


def swiglu(h, w_g, w_u, w_d):
    return (jax.nn.silu(h @ w_g) * (h @ w_u)) @ w_d


def moe_swiglu(h, w_router, w_g, w_u, w_d):
    shp = h.shape
    t = h.reshape(-1, shp[-1])
    logits = (t @ w_router).astype(jnp.float32)
    top_v, top_i = lax.top_k(logits, TOP_K)
    top_w = jax.nn.softmax(top_v, axis=-1)
    gates = jnp.sum(jax.nn.one_hot(top_i, N_EXPERTS, dtype=jnp.float32) * top_w[..., None], axis=1).astype(h.dtype)
    y = jnp.zeros_like(t)
    for e in range(N_EXPERTS):
        y = y + gates[:, e:e + 1] * swiglu(t, w_g[e], w_u[e], w_d[e])
    return y.reshape(shp)


def setup_inputs(seed: int = 0) -> dict:
    key = jax.random.key(seed)
    ks = iter(jax.random.split(key, 32))

    def nrm(shape, scale):
        return jax.random.normal(next(ks), shape, jnp.float32) * scale

    D = D_MODEL
    s = D ** -0.5
    a_qd = A_HEADS * A_HEAD_DIM
    a_kd = A_KV_HEADS * A_HEAD_DIM
    b_qd = 2 * B_HEADS * B_HEAD_DIM
    b_vd = B_HEADS * B_V_DIM
    x = nrm((BATCH, SEQ, D), 1.0)
    c = nrm((BATCH, D), 1.0)
    ctx = nrm((BATCH, CTX_LEN, D), 1.0)
    c_ctx = nrm((D,), 1.0)
    w_mod = nrm((DEPTH, D, N_MOD * D), s * MOD_INIT_SCALE)
    b_mod = nrm((DEPTH, N_MOD * D), 0.01)
    ln_g = 1.0 + nrm((DEPTH, 2, D), 0.02)
    ln_b = nrm((DEPTH, 2, D), 0.02)
    a_w_qkv = jnp.concatenate([nrm((N_EVEN, D, a_qd + a_kd), s), nrm((N_EVEN, D, a_kd), s * DEEPNORM_BETA)], axis=-1)
    a_sink = nrm((N_EVEN, A_HEADS), 0.5)
    a_w_o = nrm((N_EVEN, a_qd, D), a_qd ** -0.5 * DEEPNORM_BETA)
    b_w_qkv = jnp.concatenate([nrm((N_ODD, D, 2 * b_qd), s), nrm((N_ODD, D, b_vd), s * DEEPNORM_BETA)], axis=-1)
    b_lam_q1 = nrm((N_ODD, B_HEAD_DIM), 0.1)
    b_lam_k1 = nrm((N_ODD, B_HEAD_DIM), 0.1)
    b_lam_q2 = nrm((N_ODD, B_HEAD_DIM), 0.1)
    b_lam_k2 = nrm((N_ODD, B_HEAD_DIM), 0.1)
    b_subln_g = 1.0 + nrm((N_ODD, B_V_DIM), 0.02)
    b_w_o = nrm((N_ODD, b_vd, D), b_vd ** -0.5 * DEEPNORM_BETA)
    ff_w_gate = nrm((N_EVEN, D, D_FF), s)
    ff_w_up = nrm((N_EVEN, D, D_FF), s)
    ff_w_down = nrm((N_EVEN, D_FF, D), D_FF ** -0.5 * DEEPNORM_BETA)
    moe_w_router = nrm((N_ODD, D, N_EXPERTS), s)
    moe_w_gate = nrm((N_ODD, N_EXPERTS, D, D_FF_EXPERT), s)
    moe_w_up = nrm((N_ODD, N_EXPERTS, D, D_FF_EXPERT), s)
    moe_w_down = nrm((N_ODD, N_EXPERTS, D_FF_EXPERT, D), D_FF_EXPERT ** -0.5 * DEEPNORM_BETA)
    return {'x': x, 'c': c, 'ctx': ctx, 'c_ctx': c_ctx, 'w_mod': w_mod, 'b_mod': b_mod,
            'ln_g': ln_g, 'ln_b': ln_b, 'a_w_qkv': a_w_qkv, 'a_sink': a_sink, 'a_w_o': a_w_o,
            'b_w_qkv': b_w_qkv, 'b_lam_q1': b_lam_q1, 'b_lam_k1': b_lam_k1, 'b_lam_q2': b_lam_q2,
            'b_lam_k2': b_lam_k2, 'b_subln_g': b_subln_g, 'b_w_o': b_w_o,
            'ff_w_gate': ff_w_gate, 'ff_w_up': ff_w_up, 'ff_w_down': ff_w_down,
            'moe_w_router': moe_w_router, 'moe_w_gate': moe_w_gate, 'moe_w_up': moe_w_up,
            'moe_w_down': moe_w_down}


def reference(x, c, ctx, c_ctx, w_mod, b_mod, ln_g, ln_b, a_w_qkv, a_sink, a_w_o,
              b_w_qkv, b_lam_q1, b_lam_k1, b_lam_q2, b_lam_k2, b_subln_g, b_w_o,
              ff_w_gate, ff_w_up, ff_w_down, moe_w_router, moe_w_gate, moe_w_up, moe_w_down):
    B, L, D = x.shape
    C = ctx.shape[1]
    cos, sin = axial_rope_tables(L, A_HEAD_DIM, x.dtype)
    xl, xc = x, ctx
    for i in range(DEPTH):
        with_ctx = i < DEPTH - 1
        j = i // 2
        sh_a, sc_a, g_a, sh_f, sc_f, g_f = modulation(c, w_mod[i], b_mod[i])
        csh_a, csc_a, cg_a, csh_f, csc_f, cg_f = modulation(c_ctx, w_mod[i], b_mod[i])

        hl = xl * (1.0 + sc_a[:, None]) + sh_a[:, None]
        hc = xc * (1.0 + csc_a) + csh_a
        if i % 2 == 0:
            ol, oc = windowed_gqa(hl, hc, a_w_qkv[j], a_sink[j], a_w_o[j], cos, sin, with_ctx)
        else:
            lam_init = 0.8 - 0.6 * math.exp(-0.3 * i)
            ol, oc = diff_attention(hl, hc, b_w_qkv[j], b_lam_q1[j], b_lam_k1[j], b_lam_q2[j], b_lam_k2[j],
                                    b_subln_g[j], b_w_o[j], cos, sin, lam_init, with_ctx)
        xl = layer_norm(DEEPNORM_ALPHA * xl + (1.0 + g_a[:, None]) * ol, ln_g[i, 0], ln_b[i, 0])
        if with_ctx:
            xc = layer_norm(DEEPNORM_ALPHA * xc + (1.0 + cg_a) * oc, ln_g[i, 0], ln_b[i, 0])

        hl = xl * (1.0 + sc_f[:, None]) + sh_f[:, None]
        if with_ctx:
            hc = xc * (1.0 + csc_f) + csh_f
            h = jnp.concatenate([hc, hl], axis=1)
        else:
            h = hl
        if i % 2 == 0:
            f = swiglu(h, ff_w_gate[j], ff_w_up[j], ff_w_down[j])
        else:
            f = moe_swiglu(h, moe_w_router[j], moe_w_gate[j], moe_w_up[j], moe_w_down[j])
        if with_ctx:
            fc, fl = f[:, :C], f[:, C:]
            xc = layer_norm(DEEPNORM_ALPHA * xc + (1.0 + cg_f) * fc, ln_g[i, 1], ln_b[i, 1])
        else:
            fl = f
        xl = layer_norm(DEEPNORM_ALPHA * xl + (1.0 + g_f[:, None]) * fl, ln_g[i, 1], ln_b[i, 1])
    return xl
```

```python
import functools
import math

import numpy as np
import jax
import jax.numpy as jnp
from jax import lax
from jax.experimental import pallas as pl
from jax.experimental.pallas import tpu as pltpu

f32 = jnp.float32
bf16 = jnp.bfloat16

D = 1024
DEPTH = 4
GRID_W = 64
N_MOD = 6
HEAD_DIM = 64
A_HEADS = 16
A_KV_HEADS = 4
WINDOW = 128
B_HEADS = 8
B_V_DIM = 128
N_EXPERTS = 8
ROPE_THETA = 10000.0
LN_EPS = 1e-5
RMS_EPS = 1e-5
NEG_INF = -1e30
ALPHA = (2.0 * DEPTH) ** 0.25
LOG2E = 1.4426950408889634

LANES = 128
SUB = 256
TM = 768
NSUB = TM // SUB
QB_A = 128
QB_B = 256
VMEM_LIMIT = 52 * 1024 * 1024

SH_A, SC_A, G_A, SH_F, SC_F, G_F = range(6)


def _dot(a, b):
    return jnp.dot(a, b, preferred_element_type=f32)


def _dot_nt(a, b):
    return lax.dot_general(a, b, (((1,), (1,)), ((), ())), preferred_element_type=f32)


def _params(n_grid):
    return pltpu.CompilerParams(dimension_semantics=("arbitrary",) * n_grid,
                                vmem_limit_bytes=VMEM_LIMIT)


MOD_ROWS = 24
MOD_NB = 1536


def _mod_kernel(c_ref, w_ref, b_ref, o_ref):
    cond = c_ref[...]
    s = cond * (1.0 / (1.0 + jnp.exp(-cond)))
    o_ref[...] = _dot(s.astype(bf16), w_ref[...].astype(bf16)) + b_ref[...]


def _modulation(cond, w_mod, b_mod):
    n = N_MOD * D
    return pl.pallas_call(
        _mod_kernel,
        grid=(DEPTH, n // MOD_NB),
        in_specs=[pl.BlockSpec((MOD_ROWS, D), lambda i, j: (0, 0)),
                  pl.BlockSpec((None, D, MOD_NB), lambda i, j: (i, 0, j)),
                  pl.BlockSpec((None, 1, MOD_NB), lambda i, j: (i, 0, j))],
        out_specs=pl.BlockSpec((None, MOD_ROWS, MOD_NB), lambda i, j: (i, 0, j)),
        out_shape=jax.ShapeDtypeStruct((DEPTH, MOD_ROWS, n), f32),
        compiler_params=_params(2),
        name="modulation",
    )(cond, w_mod, b_mod.reshape(DEPTH, 1, n))


def _modulated(x_ref, mod_ref, sh, sc, dtype):
    hs = []
    for s in range(NSUB):
        xs = x_ref[s * SUB:(s + 1) * SUB, :]
        hs.append((xs * (1.0 + mod_ref[s, sc:sc + 1, :]) + mod_ref[s, sh:sh + 1, :]).astype(dtype))
    return jnp.concatenate(hs, axis=0)


def _qkv_kernel(x_ref, mod_ref, cos_ref, sa_ref, sb_ref, wq_ref, wk_ref, wv_ref,
                q_ref, k_ref, v_ref, *, qscale):
    h = _modulated(x_ref, mod_ref, SH_A, SC_A, bf16)
    cos = cos_ref[...]
    sa = sa_ref[...]
    sb = sb_ref[...]

    def rope(y):
        return (y * cos + pltpu.roll(y, LANES - 16, 1) * sa + pltpu.roll(y, 16, 1) * sb)

    yq = _dot(h, wq_ref[...])
    for j in range(yq.shape[1] // LANES):
        blk = slice(j * LANES, (j + 1) * LANES)
        q_ref[:, blk] = (rope(yq[:, blk]) * qscale).astype(bf16)
    yk = _dot(h, wk_ref[...])
    for j in range(yk.shape[1] // LANES):
        blk = slice(j * LANES, (j + 1) * LANES)
        k_ref[:, blk] = rope(yk[:, blk]).astype(bf16)
    v_ref[...] = _dot(h, wv_ref[...]).astype(bf16)


def _qkv(X, modtab, rope_tabs, wq, wk, wv, qscale):
    B, T, _ = X.shape
    nq, nk, nv = wq.shape[1], wk.shape[1], wv.shape[1]
    cos, sa, sb = rope_tabs
    tab_spec = pl.BlockSpec((TM, LANES), lambda b, t: (t, 0))
    return pl.pallas_call(
        functools.partial(_qkv_kernel, qscale=qscale),
        grid=(B, T // TM),
        in_specs=[pl.BlockSpec((None, TM, D), lambda b, t: (b, t, 0)),
                  pl.BlockSpec((None, NSUB, N_MOD, D), lambda b, t: (b, t, 0, 0)),
                  tab_spec, tab_spec, tab_spec,
                  pl.BlockSpec((D, nq), lambda b, t: (0, 0)),
                  pl.BlockSpec((D, nk), lambda b, t: (0, 0)),
                  pl.BlockSpec((D, nv), lambda b, t: (0, 0))],
        out_specs=[pl.BlockSpec((None, TM, nq), lambda b, t: (b, t, 0)),
                   pl.BlockSpec((None, TM, nk), lambda b, t: (b, t, 0)),
                   pl.BlockSpec((None, TM, nv), lambda b, t: (b, t, 0))],
        out_shape=[jax.ShapeDtypeStruct((B, T, nq), bf16),
                   jax.ShapeDtypeStruct((B, T, nk), bf16),
                   jax.ShapeDtypeStruct((B, T, nv), bf16)],
        compiler_params=_params(2),
        name="qkv_proj",
    )(X, modtab, cos, sa, sb, wq, wk, wv)


A_PAIRS = 4


def _attn_a_kernel(sink_ref, q_ref, k_ref, v_ref, o_ref, *, n_ctx, n_lat):
    hp = pl.program_id(1)
    n = pl.program_id(2)
    n_ctx_blocks = n_ctx // QB_A
    lane = lax.broadcasted_iota(jnp.int32, (QB_A, LANES), 1)
    lo = lane < HEAD_DIM
    q = q_ref[...]
    zero = jnp.zeros((QB_A, LANES), bf16)
    qparts = [q[:, p * LANES:(p + 1) * LANES] for p in range(A_PAIRS)]
    q_lo = jnp.concatenate([jnp.where(lo, qp, zero) for qp in qparts], axis=0)
    q_hi = jnp.concatenate([jnp.where(lo, zero, qp) for qp in qparts], axis=0)

    def sink_col(half):
        return jnp.concatenate(
            [jnp.full((QB_A, 1), sink_ref[hp * 2 * A_PAIRS + 2 * p + half], f32)
             for p in range(A_PAIRS)], axis=0)

    def attend(qs, sink, keys, vals, mask):
        s = _dot_nt(qs, keys)
        if mask is not None:
            s = jnp.where(mask, s, NEG_INF)
        m = jnp.maximum(jnp.max(s, axis=1, keepdims=True), sink)
        e = jnp.exp2(s - m)
        l = jnp.sum(e, axis=1, keepdims=True) + jnp.exp2(sink - m)
        return _dot(e.astype(bf16), vals) / l

    def run(keys, vals, mask):
        o_lo = attend(q_lo, sink_col(0), keys, vals, mask)
        o_hi = attend(q_hi, sink_col(1), keys, vals, mask)
        for p in range(A_PAIRS):
            rows = slice(p * QB_A, (p + 1) * QB_A)
            o_ref[:, p * LANES:(p + 1) * LANES] = jnp.where(lo, o_lo[rows], o_hi[rows]).astype(bf16)

    @pl.when(n < n_ctx_blocks)
    def _():
        run(k_ref[0:n_ctx, :], v_ref[0:n_ctx, :], None)

    @pl.when(n >= n_ctx_blocks)
    def _():
        nl = n - n_ctx_blocks
        band = 3 * QB_A
        start = jnp.clip((nl - 1) * QB_A, 0, n_lat - band)
        row0 = pl.multiple_of(n_ctx + start, QB_A)
        keys = jnp.concatenate([k_ref[0:n_ctx, :], k_ref[pl.ds(row0, band), :]], axis=0)
        vals = jnp.concatenate([v_ref[0:n_ctx, :], v_ref[pl.ds(row0, band), :]], axis=0)
        nk = n_ctx + band
        qpos = nl * QB_A + lax.broadcasted_iota(jnp.int32, (QB_A, nk), 0)
        col = lax.broadcasted_iota(jnp.int32, (QB_A, nk), 1)
        kpos = start + col - n_ctx
        ok = (col < n_ctx) | (jnp.abs(kpos - qpos) <= WINDOW)
        mask = jnp.concatenate([ok] * A_PAIRS, axis=0)
        run(keys, vals, mask)


def _attn_a(q, k, v, sink, n_ctx):
    B, T, nq = q.shape
    n_lat = T - n_ctx
    step_cols = A_PAIRS * LANES
    return pl.pallas_call(
        functools.partial(_attn_a_kernel, n_ctx=n_ctx, n_lat=n_lat),
        grid=(B, nq // step_cols, T // QB_A),
        in_specs=[pl.BlockSpec(memory_space=pltpu.SMEM),
                  pl.BlockSpec((None, QB_A, step_cols), lambda b, h, n: (b, n, h)),
                  pl.BlockSpec((None, T, LANES), lambda b, h, n: (b, 0, h)),
                  pl.BlockSpec((None, T, LANES), lambda b, h, n: (b, 0, h))],
        out_specs=pl.BlockSpec((None, QB_A, step_cols), lambda b, h, n: (b, n, h)),
        out_shape=jax.ShapeDtypeStruct((B, T, nq), bf16),
        compiler_params=_params(3),
        name="attn_window",
    )(sink, q, k, v)


def _attn_b_kernel(lam_ref, q_ref, k_ref, v_ref, g_ref, o_ref, *, n_ctx, out_scale):
    n = pl.program_id(2)
    lam = lam_ref[0]
    lane = lax.broadcasted_iota(jnp.int32, (QB_B, LANES), 1)
    lo = lane < HEAD_DIM
    q = q_ref[...]
    zero = jnp.zeros((QB_B, LANES), bf16)
    qs = jnp.concatenate([jnp.where(lo, q, zero), jnp.where(lo, zero, q)], axis=0)

    def run(keys, vals):
        s = _dot_nt(qs, keys)
        m = jnp.max(s, axis=1, keepdims=True)
        e = jnp.exp2(s - m)
        l = jnp.sum(e, axis=1, keepdims=True)
        o = _dot(e.astype(bf16), vals) / l
        od = o[0:QB_B] - lam * o[QB_B:2 * QB_B]
        y = od * lax.rsqrt(jnp.mean(od * od, axis=1, keepdims=True) + RMS_EPS)
        o_ref[...] = (y * g_ref[...] * out_scale).astype(bf16)

    @pl.when(n < n_ctx // QB_B)
    def _():
        run(k_ref[0:n_ctx, :], v_ref[0:n_ctx, :])

    @pl.when(n >= n_ctx // QB_B)
    def _():
        run(k_ref[...], v_ref[...])


def _attn_b(q, k, v, lam, subln_g, n_ctx, out_scale):
    B, T, nq = q.shape
    return pl.pallas_call(
        functools.partial(_attn_b_kernel, n_ctx=n_ctx, out_scale=out_scale),
        grid=(B, nq // LANES, T // QB_B),
        in_specs=[pl.BlockSpec(memory_space=pltpu.SMEM),
                  pl.BlockSpec((None, QB_B, LANES), lambda b, h, n: (b, n, h)),
                  pl.BlockSpec((None, T, LANES), lambda b, h, n: (b, 0, h)),
                  pl.BlockSpec((None, T, LANES), lambda b, h, n: (b, 0, h)),
                  pl.BlockSpec((1, LANES), lambda b, h, n: (0, 0))],
        out_specs=pl.BlockSpec((None, QB_B, LANES), lambda b, h, n: (b, n, h)),
        out_shape=jax.ShapeDtypeStruct((B, T, nq), bf16),
        compiler_params=_params(3),
        name="attn_diff",
    )(lam, q, k, v, subln_g)


def _residual_ln(x, y, gate, ln_g, ln_b):
    z = ALPHA * x + (1.0 + gate) * y
    mu = jnp.mean(z, axis=1, keepdims=True)
    zc = z - mu
    var = jnp.mean(zc * zc, axis=1, keepdims=True)
    return zc * lax.rsqrt(var + LN_EPS) * ln_g + ln_b


def _oproj_kernel(o_ref, x_ref, mod_ref, w_ref, g_ref, b_ref, xo_ref):
    y = _dot(o_ref[...], w_ref[...])
    for s in range(NSUB):
        rows = slice(s * SUB, (s + 1) * SUB)
        xo_ref[rows, :] = _residual_ln(x_ref[rows, :], y[rows, :], mod_ref[s, G_A:G_A + 1, :],
                                       g_ref[...], b_ref[...])


def _oproj(o, X, modtab, w_o, ln_g, ln_b):
    B, T, _ = X.shape
    row_spec = pl.BlockSpec((None, TM, D), lambda b, t: (b, t, 0))
    vec_spec = pl.BlockSpec((1, D), lambda b, t: (0, 0))
    return pl.pallas_call(
        _oproj_kernel,
        grid=(B, T // TM),
        in_specs=[row_spec, row_spec,
                  pl.BlockSpec((None, NSUB, N_MOD, D), lambda b, t: (b, t, 0, 0)),
                  pl.BlockSpec((D, D), lambda b, t: (0, 0)),
                  vec_spec, vec_spec],
        out_specs=row_spec,
        out_shape=jax.ShapeDtypeStruct((B, T, D), f32),
        compiler_params=_params(2),
        name="oproj_ln",
    )(o, X, modtab, w_o, ln_g, ln_b)


def _ffn_kernel(*refs, gated):
    if gated:
        (x_ref, mod_ref, gates_ref, wg_ref, wu_ref, wd_ref, lng_ref, lnb_ref,
         o_ref, h_scr, acc_scr) = refs
    else:
        (x_ref, mod_ref, wg_ref, wu_ref, wd_ref, lng_ref, lnb_ref,
         o_ref, h_scr, acc_scr) = refs
    e = pl.program_id(2)
    f = pl.program_id(3)
    last = (e == pl.num_programs(2) - 1) & (f == pl.num_programs(3) - 1)

    @pl.when((e == 0) & (f == 0))
    def _():
        h_scr[...] = _modulated(x_ref, mod_ref, SH_F, SC_F, bf16)
        acc_scr[...] = jnp.zeros_like(acc_scr)

    h = h_scr[...]
    g = _dot(h, wg_ref[...])
    u = _dot(h, wu_ref[...])
    a = g * (1.0 / (1.0 + jnp.exp(-g))) * u
    if gated:
        lane = lax.broadcasted_iota(jnp.int32, gates_ref.shape, 1)
        gate = jnp.sum(jnp.where(lane == e, gates_ref[...], 0.0), axis=1, keepdims=True)
        a = a * gate
    acc_scr[...] += _dot(a.astype(bf16), wd_ref[...])

    @pl.when(last)
    def _():
        for s in range(NSUB):
            rows = slice(s * SUB, (s + 1) * SUB)
            o_ref[rows, :] = _residual_ln(x_ref[rows, :], acc_scr[rows, :],
                                          mod_ref[s, G_F:G_F + 1, :], lng_ref[...], lnb_ref[...])


def _ffn(X, modtab, gates, wg, wu, wd, ln_g, ln_b, fc):
    B, T, _ = X.shape
    n_e, _, n_hidden = wg.shape
    gated = gates is not None
    row_spec = pl.BlockSpec((None, TM, D), lambda b, t, e, f: (b, t, 0))
    vec_spec = pl.BlockSpec((1, D), lambda b, t, e, f: (0, 0))
    in_specs = [row_spec, pl.BlockSpec((None, NSUB, N_MOD, D), lambda b, t, e, f: (b, t, 0, 0))]
    args = [X, modtab]
    if gated:
        in_specs.append(pl.BlockSpec((None, TM, LANES), lambda b, t, e, f: (b, t, 0)))
        args.append(gates)
    in_specs += [pl.BlockSpec((None, D, fc), lambda b, t, e, f: (e, 0, f)),
                 pl.BlockSpec((None, D, fc), lambda b, t, e, f: (e, 0, f)),
                 pl.BlockSpec((None, fc, D), lambda b, t, e, f: (e, f, 0)),
                 vec_spec, vec_spec]
    args += [wg, wu, wd, ln_g, ln_b]
    return pl.pallas_call(
        functools.partial(_ffn_kernel, gated=gated),
        grid=(B, T // TM, n_e, n_hidden // fc),
        in_specs=in_specs,
        out_specs=row_spec,
        out_shape=jax.ShapeDtypeStruct((B, T, D), f32),
        scratch_shapes=[pltpu.VMEM((TM, D), bf16), pltpu.VMEM((TM, D), f32)],
        compiler_params=_params(4),
        name="ffn_gated" if gated else "ffn_dense",
    )(*args)


def _router_kernel(x_ref, mod_ref, wr_ref, gates_ref):
    lane = lax.broadcasted_iota(jnp.int32, (SUB, LANES), 1)
    for s in range(NSUB):
        rows = slice(s * SUB, (s + 1) * SUB)
        h = x_ref[rows, :] * (1.0 + mod_ref[s, SC_F:SC_F + 1, :]) + mod_ref[s, SH_F:SH_F + 1, :]
        logits = jnp.dot(h, wr_ref[...], preferred_element_type=f32,
                         precision=lax.Precision.HIGHEST)
        lg = jnp.where(lane < N_EXPERTS, logits, -jnp.inf)
        v1 = jnp.max(lg, axis=1, keepdims=True)
        i1 = jnp.min(jnp.where(lg == v1, lane, LANES), axis=1, keepdims=True)
        lg2 = jnp.where(lane == i1, -jnp.inf, lg)
        v2 = jnp.max(lg2, axis=1, keepdims=True)
        i2 = jnp.min(jnp.where(lg2 == v2, lane, LANES), axis=1, keepdims=True)
        ex = jnp.exp(v2 - v1)
        w1 = 1.0 / (1.0 + ex)
        w2 = ex / (1.0 + ex)
        gates_ref[rows, :] = jnp.where(lane == i1, w1, jnp.where(lane == i2, w2, 0.0))


def _router(X, modtab, w_router_padded):
    B, T, _ = X.shape
    return pl.pallas_call(
        _router_kernel,
        grid=(B, T // TM),
        in_specs=[pl.BlockSpec((None, TM, D), lambda b, t: (b, t, 0)),
                  pl.BlockSpec((None, NSUB, N_MOD, D), lambda b, t: (b, t, 0, 0)),
                  pl.BlockSpec((D, LANES), lambda b, t: (0, 0))],
        out_specs=pl.BlockSpec((None, TM, LANES), lambda b, t: (b, t, 0)),
        out_shape=jax.ShapeDtypeStruct((B, T, LANES), f32),
        compiler_params=_params(2),
        name="router",
    )(X, modtab, w_router_padded)


def _rope_tables(n_ctx, n_lat):
    rows = n_lat // GRID_W
    row = jnp.repeat(jnp.arange(rows, dtype=f32), GRID_W)
    col = jnp.tile(jnp.arange(GRID_W, dtype=f32), rows)
    half = HEAD_DIM // 2
    inv = ROPE_THETA ** (-jnp.arange(0, half, 2, dtype=f32) / half)
    ang_r = row[:, None] * inv[None, :]
    ang_c = col[:, None] * inv[None, :]
    ang = jnp.concatenate([ang_r, ang_r, ang_c, ang_c], axis=-1)
    cos, sin = jnp.cos(ang), jnp.sin(ang)
    quarter = (np.arange(HEAD_DIM) // (HEAD_DIM // 4)) % 2
    sa = jnp.where(quarter == 0, -sin, 0.0)
    sb = jnp.where(quarter == 1, sin, 0.0)
    reps = LANES // HEAD_DIM

    def full(tab, ctx_value):
        tab = jnp.concatenate([jnp.full((n_ctx, HEAD_DIM), ctx_value, f32), tab], axis=0)
        return jnp.tile(tab, (1, reps))

    return full(cos, 1.0), full(sa, 0.0), full(sb, 0.0)


def _a_head_perm():
    group = A_HEADS // A_KV_HEADS
    order = []
    for hp in range(A_KV_HEADS // 2):
        for j in range(group):
            order += [(2 * hp) * group + j, (2 * hp + 1) * group + j]
    return np.asarray(order)


def kernel(x, c, ctx, c_ctx, w_mod, b_mod, ln_g, ln_b, a_w_qkv, a_sink, a_w_o,
           b_w_qkv, b_lam_q1, b_lam_k1, b_lam_q2, b_lam_k2, b_subln_g, b_w_o,
           ff_w_gate, ff_w_up, ff_w_down, moe_w_router, moe_w_gate, moe_w_up, moe_w_down):
    B, L, _ = x.shape
    C = ctx.shape[1]
    T = C + L
    X = jnp.concatenate([ctx, x], axis=1)

    cond = jnp.concatenate([c, c_ctx[None, :], jnp.zeros((MOD_ROWS - B - 1, D), f32)], axis=0)
    mods = _modulation(cond, w_mod, b_mod).reshape(DEPTH, MOD_ROWS, N_MOD, D)
    rope_tabs = _rope_tables(C, L)
    perm = _a_head_perm()
    perm_cols = (perm[:, None] * HEAD_DIM + np.arange(HEAD_DIM)[None, :]).reshape(-1)
    qscale = HEAD_DIM ** -0.5 * LOG2E
    a_qd = A_HEADS * HEAD_DIM
    a_kd = A_KV_HEADS * HEAD_DIM
    b_qd = 2 * B_HEADS * HEAD_DIM

    for i in range(DEPTH):
        j = i // 2
        m_lat = jnp.broadcast_to(mods[i, :B, None], (B, L // SUB, N_MOD, D))
        m_ctx = jnp.broadcast_to(mods[i, B][None, None], (B, C // SUB, N_MOD, D))
        modtab = jnp.concatenate([m_ctx, m_lat], axis=1)

        if i % 2 == 0:
            w = a_w_qkv[j]
            wq = w[:, :a_qd][:, perm_cols].astype(bf16)
            wk = w[:, a_qd:a_qd + a_kd].astype(bf16)
            wv = w[:, a_qd + a_kd:].astype(bf16)
            q, k, v = _qkv(X, modtab, rope_tabs, wq, wk, wv, qscale)
            sink = a_sink[j][perm].astype(f32) * LOG2E
            o = _attn_a(q, k, v, sink, C)
            w_o = a_w_o[j][perm_cols, :].astype(bf16)
        else:
            w = b_w_qkv[j]
            wq = w[:, :b_qd].astype(bf16)
            wk = w[:, b_qd:2 * b_qd].astype(bf16)
            wv = w[:, 2 * b_qd:].astype(bf16)
            q, k, v = _qkv(X, modtab, rope_tabs, wq, wk, wv, qscale)
            lam_init = 0.8 - 0.6 * math.exp(-0.3 * i)
            lam = (jnp.exp(jnp.sum(b_lam_q1[j] * b_lam_k1[j]))
                   - jnp.exp(jnp.sum(b_lam_q2[j] * b_lam_k2[j])) + lam_init).reshape(1)
            o = _attn_b(q, k, v, lam, b_subln_g[j].reshape(1, B_V_DIM), C, 1.0 - lam_init)
            w_o = b_w_o[j].astype(bf16)
        X = _oproj(o, X, modtab, w_o, ln_g[i, 0].reshape(1, D), ln_b[i, 0].reshape(1, D))

        lng = ln_g[i, 1].reshape(1, D)
        lnb = ln_b[i, 1].reshape(1, D)
        if i % 2 == 0:
            X = _ffn(X, modtab, None, ff_w_gate[j][None].astype(bf16), ff_w_up[j][None].astype(bf16),
                     ff_w_down[j][None].astype(bf16), lng, lnb, fc=1408)
        else:
            wr = jnp.pad(moe_w_router[j], ((0, 0), (0, LANES - N_EXPERTS)))
            gates = _router(X, modtab, wr)
            X = _ffn(X, modtab, gates, moe_w_gate[j].astype(bf16), moe_w_up[j].astype(bf16),
                     moe_w_down[j].astype(bf16), lng, lnb, fc=1792)
    return X[:, C:, :]
```

```python
import functools
import math

import numpy as np
import jax
import jax.numpy as jnp
from jax import lax
from jax.experimental import pallas as pl
from jax.experimental.pallas import tpu as pltpu

f32 = jnp.float32
bf16 = jnp.bfloat16

D = 1024
DEPTH = 4
GRID_W = 64
N_MOD = 6
HEAD_DIM = 64
A_HEADS = 16
A_KV_HEADS = 4
WINDOW = 128
B_HEADS = 8
B_V_DIM = 128
N_EXPERTS = 8
ROPE_THETA = 10000.0
LN_EPS = 1e-5
RMS_EPS = 1e-5
NEG_INF = -1e30
ALPHA = (2.0 * DEPTH) ** 0.25
LOG2E = 1.4426950408889634

LANES = 128
SUB = 256
TM = 768
NSUB = TM // SUB
QB = 128
KB = 128
QB_B = 256
VMEM_LIMIT = 52 * 1024 * 1024

SH_A, SC_A, G_A, SH_F, SC_F, G_F = range(6)


def _dot(a, b):
    return jnp.dot(a, b, preferred_element_type=f32)


def _params(n_grid):
    return pltpu.CompilerParams(dimension_semantics=("arbitrary",) * n_grid,
                                vmem_limit_bytes=VMEM_LIMIT)


MOD_ROWS = 24
MOD_NB = 1536


def _mod_kernel(c_ref, w_ref, b_ref, o_ref):
    cond = c_ref[...]
    s = cond * (1.0 / (1.0 + jnp.exp(-cond)))
    o_ref[...] = _dot(s.astype(bf16), w_ref[...].astype(bf16)) + b_ref[...]


def _modulation(cond, w_mod, b_mod):
    n = N_MOD * D
    return pl.pallas_call(
        _mod_kernel,
        grid=(DEPTH, n // MOD_NB),
        in_specs=[pl.BlockSpec((MOD_ROWS, D), lambda i, j: (0, 0)),
                  pl.BlockSpec((None, D, MOD_NB), lambda i, j: (i, 0, j)),
                  pl.BlockSpec((None, 1, MOD_NB), lambda i, j: (i, 0, j))],
        out_specs=pl.BlockSpec((None, MOD_ROWS, MOD_NB), lambda i, j: (i, 0, j)),
        out_shape=jax.ShapeDtypeStruct((DEPTH, MOD_ROWS, n), f32),
        compiler_params=_params(2),
        name="modulation",
    )(cond, w_mod, b_mod.reshape(DEPTH, 1, n))


def _modulated(x_ref, mod_ref, sh, sc, dtype):
    hs = []
    for s in range(NSUB):
        xs = x_ref[s * SUB:(s + 1) * SUB, :]
        hs.append((xs * (1.0 + mod_ref[s, sc:sc + 1, :]) + mod_ref[s, sh:sh + 1, :]).astype(dtype))
    return jnp.concatenate(hs, axis=0)


def _qkv_kernel(x_ref, mod_ref, cos_ref, sa_ref, sb_ref, wq_ref, wk_ref, wv_ref,
                q_ref, kt_ref, v_ref, *, qscale):
    h = _modulated(x_ref, mod_ref, SH_A, SC_A, bf16)
    cos = cos_ref[...]
    sa = sa_ref[...]
    sb = sb_ref[...]

    def rope(y):
        return (y * cos + pltpu.roll(y, LANES - 16, 1) * sa + pltpu.roll(y, 16, 1) * sb)

    yq = _dot(h, wq_ref[...])
    for j in range(yq.shape[1] // LANES):
        blk = slice(j * LANES, (j + 1) * LANES)
        q_ref[:, blk] = (rope(yq[:, blk]) * qscale).astype(bf16)
    yk = _dot(h, wk_ref[...])
    for j in range(yk.shape[1] // LANES):
        kr = rope(yk[:, j * LANES:(j + 1) * LANES])
        for r in range(TM // KB):
            kt_ref[j, r] = kr[r * KB:(r + 1) * KB, :].T.astype(bf16)
    v_ref[...] = _dot(h, wv_ref[...]).astype(bf16)


def _qkv(X, modtab, rope_tabs, wq, wk, wv, qscale):
    B, T, _ = X.shape
    nq, nk, nv = wq.shape[1], wk.shape[1], wv.shape[1]
    cos, sa, sb = rope_tabs
    tab_spec = pl.BlockSpec((TM, LANES), lambda b, t: (t, 0))
    return pl.pallas_call(
        functools.partial(_qkv_kernel, qscale=qscale),
        grid=(B, T // TM),
        in_specs=[pl.BlockSpec((None, TM, D), lambda b, t: (b, t, 0)),
                  pl.BlockSpec((None, NSUB, N_MOD, D), lambda b, t: (b, t, 0, 0)),
                  tab_spec, tab_spec, tab_spec,
                  pl.BlockSpec((D, nq), lambda b, t: (0, 0)),
                  pl.BlockSpec((D, nk), lambda b, t: (0, 0)),
                  pl.BlockSpec((D, nv), lambda b, t: (0, 0))],
        out_specs=[pl.BlockSpec((None, TM, nq), lambda b, t: (b, t, 0)),
                   pl.BlockSpec((None, nk // LANES, TM // KB, LANES, KB), lambda b, t: (b, 0, t, 0, 0)),
                   pl.BlockSpec((None, TM, nv), lambda b, t: (b, t, 0))],
        out_shape=[jax.ShapeDtypeStruct((B, T, nq), bf16),
                   jax.ShapeDtypeStruct((B, nk // LANES, T // KB, LANES, KB), bf16),
                   jax.ShapeDtypeStruct((B, T, nv), bf16)],
        compiler_params=_params(2),
        name="qkv_proj",
    )(X, modtab, cos, sa, sb, wq, wk, wv)


A_PAIRS = 4


def _chain_update(state, q, kt, v, mask):
    s = _dot(q, kt)
    if mask is not None:
        s = jnp.where(mask, s, NEG_INF)
    if state is None:
        m_new = jnp.max(s, axis=1, keepdims=True)
        p = jnp.exp2(s - m_new)
        return m_new, jnp.sum(p, axis=1, keepdims=True), _dot(p.astype(bf16), v)
    m, l, acc = state
    m_new = jnp.maximum(m, jnp.max(s, axis=1, keepdims=True))
    alpha = jnp.exp2(m - m_new)
    p = jnp.exp2(s - m_new)
    return (m_new, alpha * l + jnp.sum(p, axis=1, keepdims=True),
            alpha * acc + _dot(p.astype(bf16), v))


def _half_masked(q):
    lo = lax.broadcasted_iota(jnp.int32, q.shape, 1) < HEAD_DIM
    zero = jnp.zeros_like(q)
    return [jnp.where(lo, q, zero), jnp.where(lo, zero, q)]


def _attn_a_kernel(sink_ref, q_ref, kt_ref, v_ref, o_ref, *, n_ctx, n_lat):
    hp = pl.program_id(1)
    n = pl.program_id(2)
    n_ctx_tiles = n_ctx // KB
    lo = lax.broadcasted_iota(jnp.int32, (QB, LANES), 1) < HEAD_DIM
    q = q_ref[...]
    chains = []
    for p in range(A_PAIRS):
        chains += _half_masked(q[:, p * LANES:(p + 1) * LANES])

    def init(i):
        return (jnp.full((QB, 1), sink_ref[hp * 2 * A_PAIRS + i], f32),
                jnp.ones((QB, 1), f32), jnp.zeros((QB, LANES), f32))

    def finish(states):
        for p in range(A_PAIRS):
            (_, l0, a0), (_, l1, a1) = states[2 * p], states[2 * p + 1]
            o_ref[:, p * LANES:(p + 1) * LANES] = jnp.where(lo, a0 * (1.0 / l0),
                                                          a1 * (1.0 / l1)).astype(bf16)

    ctx_kt = jnp.concatenate([kt_ref[j] for j in range(n_ctx_tiles)], axis=1)
    ctx_v = v_ref[0:n_ctx, :]

    @pl.when(n < n_ctx // QB)
    def _():
        finish([_chain_update(init(i), chains[i], ctx_kt, ctx_v, None)
                for i in range(len(chains))])

    @pl.when(n >= n_ctx // QB)
    def _():
        nl = n - n_ctx // QB
        first = jnp.clip(nl - 1, 0, n_lat // KB - 3)
        jb = n_ctx_tiles + first
        row0 = pl.multiple_of(n_ctx + first * KB, KB)
        kt01 = jnp.concatenate([kt_ref[jb], kt_ref[jb + 1]], axis=1)
        v01 = v_ref[pl.ds(row0, 2 * KB), :]
        kt2 = kt_ref[jb + 2]
        v2 = v_ref[pl.ds(row0 + 2 * KB, KB), :]
        rel = (first * KB + lax.broadcasted_iota(jnp.int32, (QB, 3 * KB), 1)
               - nl * QB - lax.broadcasted_iota(jnp.int32, (QB, 3 * KB), 0))
        ok = jnp.abs(rel) <= WINDOW
        states = []
        for i in range(len(chains)):
            st = _chain_update(init(i), chains[i], ctx_kt, ctx_v, None)
            st = _chain_update(st, chains[i], kt01, v01, ok[:, :2 * KB])
            st = _chain_update(st, chains[i], kt2, v2, ok[:, 2 * KB:])
            states.append(st)
        finish(states)


def _attn_a(q, kt, v, sink, n_ctx):
    B, T, nq = q.shape
    n_lat = T - n_ctx
    step_cols = A_PAIRS * LANES
    return pl.pallas_call(
        functools.partial(_attn_a_kernel, n_ctx=n_ctx, n_lat=n_lat),
        grid=(B, nq // step_cols, T // QB),
        in_specs=[pl.BlockSpec(memory_space=pltpu.SMEM),
                  pl.BlockSpec((None, QB, step_cols), lambda b, h, n: (b, n, h)),
                  pl.BlockSpec((None, None, T // KB, LANES, KB), lambda b, h, n: (b, h, 0, 0, 0)),
                  pl.BlockSpec((None, T, LANES), lambda b, h, n: (b, 0, h))],
        out_specs=pl.BlockSpec((None, QB, step_cols), lambda b, h, n: (b, n, h)),
        out_shape=jax.ShapeDtypeStruct((B, T, nq), bf16),
        compiler_params=_params(3),
        name="attn_window",
    )(sink, q, kt, v)


B_CHUNK_TILES = 2
B_STEP_HEADS = 2


def _attn_b_kernel(lam_ref, q_ref, kt_ref, v_ref, g_ref, o_ref, *, n_ctx, out_scale):
    n = pl.program_id(2)
    lam = lam_ref[0]
    n_tiles = kt_ref.shape[1]
    q = q_ref[...]
    units = [(h, r) for h in range(B_STEP_HEADS) for r in range(QB_B // QB)]
    chains = []
    for h, r in units:
        chains += _half_masked(q[r * QB:(r + 1) * QB, h * LANES:(h + 1) * LANES])

    def run(n_key_tiles):
        states = [None] * len(chains)
        for t0 in range(0, n_key_tiles, B_CHUNK_TILES):
            t1 = min(t0 + B_CHUNK_TILES, n_key_tiles)
            for u, (h, r) in enumerate(units):
                kt = jnp.concatenate([kt_ref[h, j] for j in range(t0, t1)], axis=1)
                v = v_ref[t0 * KB:t1 * KB, h * LANES:(h + 1) * LANES]
                for i in (2 * u, 2 * u + 1):
                    states[i] = _chain_update(states[i], chains[i], kt, v, None)
        for u, (h, r) in enumerate(units):
            (_, l1, a1), (_, l2, a2) = states[2 * u], states[2 * u + 1]
            od = a1 * (1.0 / l1) - lam * (a2 * (1.0 / l2))
            y = od * lax.rsqrt(jnp.mean(od * od, axis=1, keepdims=True) + RMS_EPS)
            o_ref[r * QB:(r + 1) * QB, h * LANES:(h + 1) * LANES] = (
                y * g_ref[...] * out_scale).astype(bf16)

    @pl.when(n < n_ctx // QB_B)
    def _():
        run(n_ctx // KB)

    @pl.when(n >= n_ctx // QB_B)
    def _():
        run(n_tiles)


def _attn_b(q, kt, v, lam, subln_g, n_ctx, out_scale):
    B, T, nq = q.shape
    cols = B_STEP_HEADS * LANES
    return pl.pallas_call(
        functools.partial(_attn_b_kernel, n_ctx=n_ctx, out_scale=out_scale),
        grid=(B, nq // cols, T // QB_B),
        in_specs=[pl.BlockSpec(memory_space=pltpu.SMEM),
                  pl.BlockSpec((None, QB_B, cols), lambda b, h, n: (b, n, h)),
                  pl.BlockSpec((None, B_STEP_HEADS, T // KB, LANES, KB), lambda b, h, n: (b, h, 0, 0, 0)),
                  pl.BlockSpec((None, T, cols), lambda b, h, n: (b, 0, h)),
                  pl.BlockSpec((1, LANES), lambda b, h, n: (0, 0))],
        out_specs=pl.BlockSpec((None, QB_B, cols), lambda b, h, n: (b, n, h)),
        out_shape=jax.ShapeDtypeStruct((B, T, nq), bf16),
        compiler_params=_params(3),
        name="attn_diff",
    )(lam, q, kt, v, subln_g)


def _residual_ln(x, y, gate, ln_g, ln_b):
    z = ALPHA * x + (1.0 + gate) * y
    mu = jnp.mean(z, axis=1, keepdims=True)
    zc = z - mu
    var = jnp.mean(zc * zc, axis=1, keepdims=True)
    return zc * lax.rsqrt(var + LN_EPS) * ln_g + ln_b


def _oproj_kernel(o_ref, x_ref, mod_ref, w_ref, g_ref, b_ref, xo_ref):
    y = _dot(o_ref[...], w_ref[...])
    for s in range(NSUB):
        rows = slice(s * SUB, (s + 1) * SUB)
        xo_ref[rows, :] = _residual_ln(x_ref[rows, :], y[rows, :], mod_ref[s, G_A:G_A + 1, :],
                                       g_ref[...], b_ref[...])


def _oproj(o, X, modtab, w_o, ln_g, ln_b):
    B, T, _ = X.shape
    row_spec = pl.BlockSpec((None, TM, D), lambda b, t: (b, t, 0))
    vec_spec = pl.BlockSpec((1, D), lambda b, t: (0, 0))
    return pl.pallas_call(
        _oproj_kernel,
        grid=(B, T // TM),
        in_specs=[row_spec, row_spec,
                  pl.BlockSpec((None, NSUB, N_MOD, D), lambda b, t: (b, t, 0, 0)),
                  pl.BlockSpec((D, D), lambda b, t: (0, 0)),
                  vec_spec, vec_spec],
        out_specs=row_spec,
        out_shape=jax.ShapeDtypeStruct((B, T, D), f32),
        compiler_params=_params(2),
        name="oproj_ln",
    )(o, X, modtab, w_o, ln_g, ln_b)


FFN_FC = 1408


def _ffn_kernel(x_ref, mod_ref, wg_ref, wu_ref, wd_ref, lng_ref, lnb_ref, o_ref, h_scr, acc_scr):
    f = pl.program_id(2)

    @pl.when(f == 0)
    def _():
        h_scr[...] = _modulated(x_ref, mod_ref, SH_F, SC_F, bf16)
        acc_scr[...] = jnp.zeros_like(acc_scr)

    h = h_scr[...]
    g = _dot(h, wg_ref[...])
    u = _dot(h, wu_ref[...])
    a = g * (1.0 / (1.0 + jnp.exp(-g))) * u
    acc_scr[...] += _dot(a.astype(bf16), wd_ref[...])

    @pl.when(f == pl.num_programs(2) - 1)
    def _():
        for s in range(NSUB):
            rows = slice(s * SUB, (s + 1) * SUB)
            o_ref[rows, :] = _residual_ln(x_ref[rows, :], acc_scr[rows, :],
                                          mod_ref[s, G_F:G_F + 1, :], lng_ref[...], lnb_ref[...])


def _ffn(X, modtab, wg, wu, wd, ln_g, ln_b):
    B, T, _ = X.shape
    n_hidden = wg.shape[1]
    row_spec = pl.BlockSpec((None, TM, D), lambda b, t, f: (b, t, 0))
    vec_spec = pl.BlockSpec((1, D), lambda b, t, f: (0, 0))
    return pl.pallas_call(
        _ffn_kernel,
        grid=(B, T // TM, n_hidden // FFN_FC),
        in_specs=[row_spec, pl.BlockSpec((None, NSUB, N_MOD, D), lambda b, t, f: (b, t, 0, 0)),
                  pl.BlockSpec((D, FFN_FC), lambda b, t, f: (0, f)),
                  pl.BlockSpec((D, FFN_FC), lambda b, t, f: (0, f)),
                  pl.BlockSpec((FFN_FC, D), lambda b, t, f: (f, 0)),
                  vec_spec, vec_spec],
        out_specs=row_spec,
        out_shape=jax.ShapeDtypeStruct((B, T, D), f32),
        scratch_shapes=[pltpu.VMEM((TM, D), bf16), pltpu.VMEM((TM, D), f32)],
        compiler_params=_params(3),
        name="ffn_dense",
    )(X, modtab, wg, wu, wd, ln_g, ln_b)


def _router_kernel(x_ref, mod_ref, wr_ref, tri_ref, h_ref, gates_ref, slots_ref, cnt_ref, carry_scr):
    t = pl.program_id(1)

    @pl.when(t == 0)
    def _():
        carry_scr[...] = jnp.zeros_like(carry_scr)

    lane = lax.broadcasted_iota(jnp.int32, (SUB, LANES), 1)
    starts = []
    for s in range(NSUB):
        rows = slice(s * SUB, (s + 1) * SUB)
        h = x_ref[rows, :] * (1.0 + mod_ref[s, SC_F:SC_F + 1, :]) + mod_ref[s, SH_F:SH_F + 1, :]
        h_ref[rows, :] = h.astype(bf16)
        logits = jnp.dot(h, wr_ref[...], preferred_element_type=f32,
                         precision=lax.Precision.HIGHEST)
        lg = jnp.where(lane < N_EXPERTS, logits, -jnp.inf)
        v1 = jnp.max(lg, axis=1, keepdims=True)
        i1 = jnp.min(jnp.where(lg == v1, lane, LANES), axis=1, keepdims=True)
        lg2 = jnp.where(lane == i1, -jnp.inf, lg)
        v2 = jnp.max(lg2, axis=1, keepdims=True)
        i2 = jnp.min(jnp.where(lg2 == v2, lane, LANES), axis=1, keepdims=True)
        ex = jnp.exp(v2 - v1)
        w1 = 1.0 / (1.0 + ex)
        w2 = ex / (1.0 + ex)
        sel = (lane == i1) | (lane == i2)
        gates_ref[rows, :] = jnp.where(lane == i1, w1, jnp.where(lane == i2, w2, 0.0))
        cum = _dot(tri_ref[...], jnp.where(sel, 1.0, 0.0).astype(bf16))
        carry = carry_scr[...]
        starts.append(carry)
        slots_ref[rows, :] = jnp.where(sel, carry + cum - 1.0, -1.0)
        carry_scr[...] = carry + cum[SUB - 1:SUB, :]
    pad = jnp.zeros((8 - NSUB - 1, LANES), f32)
    cnt_ref[...] = jnp.concatenate(starts + [carry_scr[...], pad], axis=0)


def _router(X, modtab, w_router_padded, tri):
    B, T, _ = X.shape
    nt = T // TM
    row128 = pl.BlockSpec((None, TM, LANES), lambda b, t: (b, t, 0))
    return pl.pallas_call(
        _router_kernel,
        grid=(B, nt),
        in_specs=[pl.BlockSpec((None, TM, D), lambda b, t: (b, t, 0)),
                  pl.BlockSpec((None, NSUB, N_MOD, D), lambda b, t: (b, t, 0, 0)),
                  pl.BlockSpec((D, LANES), lambda b, t: (0, 0)),
                  pl.BlockSpec((SUB, SUB), lambda b, t: (0, 0))],
        out_specs=[pl.BlockSpec((None, TM, D), lambda b, t: (b, t, 0)), row128, row128,
                   pl.BlockSpec((None, None, 8, LANES), lambda b, t: (b, t, 0, 0))],
        out_shape=[jax.ShapeDtypeStruct((B, T, D), bf16),
                   jax.ShapeDtypeStruct((B, T, LANES), f32),
                   jax.ShapeDtypeStruct((B, T, LANES), f32),
                   jax.ShapeDtypeStruct((B, nt, 8, LANES), f32)],
        scratch_shapes=[pltpu.VMEM((1, LANES), f32)],
        compiler_params=_params(2),
        name="router",
    )(X, modtab, w_router_padded, tri)


CH = 128
MOE_FC = 896


def _moe_kernel(nch_ref, klo_ref, khi_ref, h_ref, slots_ref, slots_t_ref, gates_t_ref,
                wg_ref, wu_ref, wd_ref, y_ref, xs_scr, ys_scr, gate_scr, *, max_chunks):
    b = pl.program_id(0)
    e = pl.program_id(1)
    f = pl.program_id(2)
    be = b * pl.num_programs(1) + e
    nch = nch_ref[be]

    @pl.when((e == 0) & (f == 0))
    def _():
        y_ref[...] = jnp.zeros_like(y_ref)

    @pl.when(f == 0)
    def _gather():
        def chunk(c, carry):
            rows = pl.ds(pl.multiple_of(c * CH, CH), CH)
            rank = (lax.broadcasted_iota(jnp.int32, (CH, SUB), 0) + c * CH).astype(f32)

            def group(kb, acc):
                xacc, gacc = acc
                t0 = pl.multiple_of(kb * SUB, SUB)
                hit = slots_t_ref[e, pl.ds(kb, 1), :] == rank
                xacc = xacc + _dot(jnp.where(hit, 1.0, 0.0).astype(bf16), h_ref[pl.ds(t0, SUB), :])
                gacc = gacc + jnp.sum(jnp.where(hit, gates_t_ref[e, pl.ds(kb, 1), :], 0.0),
                                      axis=1, keepdims=True)
                return xacc, gacc

            xacc, gacc = lax.fori_loop(
                klo_ref[be * max_chunks + c], khi_ref[be * max_chunks + c] + 1, group,
                (jnp.zeros((CH, D), f32), jnp.zeros((CH, 1), f32)))
            xs_scr[rows, :] = xacc.astype(bf16)
            gate_scr[rows, :] = gacc
            ys_scr[rows, :] = jnp.zeros((CH, D), f32)
            return carry

        lax.fori_loop(0, nch, chunk, 0)

    def expert_rows(r0, m):
        rows = pl.ds(r0, m)
        xs = xs_scr[rows, :]
        g = _dot(xs, wg_ref[...])
        u = _dot(xs, wu_ref[...])
        a = g * (1.0 / (1.0 + jnp.exp(-g))) * u * gate_scr[rows, :]
        ys_scr[rows, :] += _dot(a.astype(bf16), wd_ref[...])

    def pair(i, carry):
        expert_rows(pl.multiple_of(i * 2 * CH, 2 * CH), 2 * CH)
        return carry

    lax.fori_loop(0, nch // 2, pair, 0)

    @pl.when(nch % 2 == 1)
    def _():
        expert_rows(pl.multiple_of((nch - 1) * CH, CH), CH)

    @pl.when(f == pl.num_programs(2) - 1)
    def _scatter():
        lane_e = lax.broadcasted_iota(jnp.int32, (SUB, LANES), 1)
        lane_r = lax.broadcasted_iota(jnp.int32, (SUB, CH), 1)

        def chunk(c, carry):
            yb = ys_scr[pl.ds(pl.multiple_of(c * CH, CH), CH), :].astype(bf16)
            rank = (lane_r + c * CH).astype(f32)

            def group(kb, carry2):
                rows = pl.ds(pl.multiple_of(kb * SUB, SUB), SUB)
                col = jnp.sum(jnp.where(lane_e == e, slots_ref[rows, :], 0.0), axis=1, keepdims=True)
                y_ref[rows, :] += _dot(jnp.where(col == rank, 1.0, 0.0).astype(bf16), yb)
                return carry2

            lax.fori_loop(klo_ref[be * max_chunks + c], khi_ref[be * max_chunks + c] + 1, group, 0)
            return carry

        lax.fori_loop(0, nch, chunk, 0)


def _moe(h, slots, slots_t, gates_t, nch, klo, khi, wg, wu, wd):
    B, T, _ = h.shape
    n_e, _, n_hidden = wg.shape
    max_chunks = T // CH
    n_groups = T // SUB
    once = pl.Buffered(1)
    grid_spec = pltpu.PrefetchScalarGridSpec(
        num_scalar_prefetch=3,
        grid=(B, n_e, n_hidden // MOE_FC),
        in_specs=[pl.BlockSpec((None, T, D), lambda b, e, f, *_: (b, 0, 0), pipeline_mode=once),
                  pl.BlockSpec((None, T, LANES), lambda b, e, f, *_: (b, 0, 0), pipeline_mode=once),
                  pl.BlockSpec((None, n_e, n_groups, SUB), lambda b, e, f, *_: (b, 0, 0, 0)),
                  pl.BlockSpec((None, n_e, n_groups, SUB), lambda b, e, f, *_: (b, 0, 0, 0)),
                  pl.BlockSpec((None, D, MOE_FC), lambda b, e, f, *_: (e, 0, f)),
                  pl.BlockSpec((None, D, MOE_FC), lambda b, e, f, *_: (e, 0, f)),
                  pl.BlockSpec((None, MOE_FC, D), lambda b, e, f, *_: (e, f, 0))],
        out_specs=pl.BlockSpec((None, T, D), lambda b, e, f, *_: (b, 0, 0), pipeline_mode=once),
        scratch_shapes=[pltpu.VMEM((T, D), bf16), pltpu.VMEM((T, D), f32),
                        pltpu.VMEM((T, 1), f32)])
    return pl.pallas_call(
        functools.partial(_moe_kernel, max_chunks=max_chunks),
        grid_spec=grid_spec,
        out_shape=jax.ShapeDtypeStruct((B, T, D), f32),
        compiler_params=_params(3),
        name="moe_routed",
    )(nch, klo, khi, h, slots, slots_t, gates_t, wg, wu, wd)


def _routing_tables(slots, gates, cnt):
    B, T, _ = slots.shape
    n_groups = T // SUB
    max_chunks = T // CH

    def expert_major(a):
        return a[:, :, :N_EXPERTS].reshape(B, n_groups, SUB, N_EXPERTS).transpose(0, 3, 1, 2)

    starts = cnt[:, :, :NSUB, :N_EXPERTS].reshape(B, n_groups, N_EXPERTS).astype(jnp.int32)
    total = cnt[:, -1, NSUB, :N_EXPERTS].astype(jnp.int32)
    nch = (total + CH - 1) // CH
    r0 = jnp.arange(max_chunks, dtype=jnp.int32)[None, None, :] * CH
    r1 = jnp.minimum(total[:, :, None], r0 + CH) - 1
    st = starts.transpose(0, 2, 1)[:, :, None, :]
    klo = jnp.sum(st <= r0[..., None], axis=-1) - 1
    khi = jnp.sum(st <= r1[..., None], axis=-1) - 1
    live = r0 < total[:, :, None]
    klo = jnp.where(live, klo, 0).astype(jnp.int32)
    khi = jnp.where(live, khi, -1).astype(jnp.int32)
    return (expert_major(slots), expert_major(gates), nch.reshape(-1),
            klo.reshape(-1), khi.reshape(-1))


def _resid_ln_kernel(x_ref, y_ref, mod_ref, g_ref, b_ref, o_ref):
    for s in range(NSUB):
        rows = slice(s * SUB, (s + 1) * SUB)
        o_ref[rows, :] = _residual_ln(x_ref[rows, :], y_ref[rows, :], mod_ref[s, G_F:G_F + 1, :],
                                      g_ref[...], b_ref[...])


def _resid_ln(X, y, modtab, ln_g, ln_b):
    B, T, _ = X.shape
    row_spec = pl.BlockSpec((None, TM, D), lambda b, t: (b, t, 0))
    vec_spec = pl.BlockSpec((1, D), lambda b, t: (0, 0))
    return pl.pallas_call(
        _resid_ln_kernel,
        grid=(B, T // TM),
        in_specs=[row_spec, row_spec,
                  pl.BlockSpec((None, NSUB, N_MOD, D), lambda b, t: (b, t, 0, 0)),
                  vec_spec, vec_spec],
        out_specs=row_spec,
        out_shape=jax.ShapeDtypeStruct((B, T, D), f32),
        compiler_params=_params(2),
        name="resid_ln",
    )(X, y, modtab, ln_g, ln_b)


def _rope_tables(n_ctx, n_lat):
    rows = n_lat // GRID_W
    row = jnp.repeat(jnp.arange(rows, dtype=f32), GRID_W)
    col = jnp.tile(jnp.arange(GRID_W, dtype=f32), rows)
    half = HEAD_DIM // 2
    inv = ROPE_THETA ** (-jnp.arange(0, half, 2, dtype=f32) / half)
    ang_r = row[:, None] * inv[None, :]
    ang_c = col[:, None] * inv[None, :]
    ang = jnp.concatenate([ang_r, ang_r, ang_c, ang_c], axis=-1)
    cos, sin = jnp.cos(ang), jnp.sin(ang)
    quarter = (np.arange(HEAD_DIM) // (HEAD_DIM // 4)) % 2
    sa = jnp.where(quarter == 0, -sin, 0.0)
    sb = jnp.where(quarter == 1, sin, 0.0)
    reps = LANES // HEAD_DIM

    def full(tab, ctx_value):
        tab = jnp.concatenate([jnp.full((n_ctx, HEAD_DIM), ctx_value, f32), tab], axis=0)
        return jnp.tile(tab, (1, reps))

    return full(cos, 1.0), full(sa, 0.0), full(sb, 0.0)


def _a_head_perm():
    group = A_HEADS // A_KV_HEADS
    order = []
    for hp in range(A_KV_HEADS // 2):
        for j in range(group):
            order += [(2 * hp) * group + j, (2 * hp + 1) * group + j]
    return np.asarray(order)


def kernel(x, c, ctx, c_ctx, w_mod, b_mod, ln_g, ln_b, a_w_qkv, a_sink, a_w_o,
           b_w_qkv, b_lam_q1, b_lam_k1, b_lam_q2, b_lam_k2, b_subln_g, b_w_o,
           ff_w_gate, ff_w_up, ff_w_down, moe_w_router, moe_w_gate, moe_w_up, moe_w_down):
    B, L, _ = x.shape
    C = ctx.shape[1]
    T = C + L
    X = jnp.concatenate([ctx, x], axis=1)

    cond = jnp.concatenate([c, c_ctx[None, :], jnp.zeros((MOD_ROWS - B - 1, D), f32)], axis=0)
    mods = _modulation(cond, w_mod, b_mod).reshape(DEPTH, MOD_ROWS, N_MOD, D)
    rope_tabs = _rope_tables(C, L)
    tri = jnp.asarray(np.tril(np.ones((SUB, SUB), np.float32)), bf16)
    perm = _a_head_perm()
    perm_cols = (perm[:, None] * HEAD_DIM + np.arange(HEAD_DIM)[None, :]).reshape(-1)
    qscale = HEAD_DIM ** -0.5 * LOG2E
    a_qd = A_HEADS * HEAD_DIM
    a_kd = A_KV_HEADS * HEAD_DIM
    b_qd = 2 * B_HEADS * HEAD_DIM

    for i in range(DEPTH):
        j = i // 2
        m_lat = jnp.broadcast_to(mods[i, :B, None], (B, L // SUB, N_MOD, D))
        m_ctx = jnp.broadcast_to(mods[i, B][None, None], (B, C // SUB, N_MOD, D))
        modtab = jnp.concatenate([m_ctx, m_lat], axis=1)

        if i % 2 == 0:
            w = a_w_qkv[j]
            wq = w[:, :a_qd][:, perm_cols].astype(bf16)
            wk = w[:, a_qd:a_qd + a_kd].astype(bf16)
            wv = w[:, a_qd + a_kd:].astype(bf16)
            q, k, v = _qkv(X, modtab, rope_tabs, wq, wk, wv, qscale)
            sink = a_sink[j][perm].astype(f32) * LOG2E
            o = _attn_a(q, k, v, sink, C)
            w_o = a_w_o[j][perm_cols, :].astype(bf16)
        else:
            w = b_w_qkv[j]
            wq = w[:, :b_qd].astype(bf16)
            wk = w[:, b_qd:2 * b_qd].astype(bf16)
            wv = w[:, 2 * b_qd:].astype(bf16)
            q, k, v = _qkv(X, modtab, rope_tabs, wq, wk, wv, qscale)
            lam_init = 0.8 - 0.6 * math.exp(-0.3 * i)
            lam = (jnp.exp(jnp.sum(b_lam_q1[j] * b_lam_k1[j]))
                   - jnp.exp(jnp.sum(b_lam_q2[j] * b_lam_k2[j])) + lam_init).reshape(1)
            o = _attn_b(q, k, v, lam, b_subln_g[j].reshape(1, B_V_DIM), C, 1.0 - lam_init)
            w_o = b_w_o[j].astype(bf16)
        X = _oproj(o, X, modtab, w_o, ln_g[i, 0].reshape(1, D), ln_b[i, 0].reshape(1, D))

        lng = ln_g[i, 1].reshape(1, D)
        lnb = ln_b[i, 1].reshape(1, D)
        if i % 2 == 0:
            X = _ffn(X, modtab, ff_w_gate[j].astype(bf16), ff_w_up[j].astype(bf16),
                     ff_w_down[j].astype(bf16), lng, lnb)
        else:
            wr = jnp.pad(moe_w_router[j], ((0, 0), (0, LANES - N_EXPERTS)))
            h, gates, slots, cnt = _router(X, modtab, wr, tri)
            slots_t, gates_t, nch, klo, khi = _routing_tables(slots, gates, cnt)
            y = _moe(h, slots, slots_t, gates_t, nch, klo, khi, moe_w_gate[j].astype(bf16),
                     moe_w_up[j].astype(bf16), moe_w_down[j].astype(bf16))
            X = _resid_ln(X, y, modtab, lng, lnb)
    return X[:, C:, :]
```

```python
import functools
import math

import numpy as np
import jax
import jax.numpy as jnp
from jax import lax
from jax.experimental import pallas as pl
from jax.experimental.pallas import tpu as pltpu

f32 = jnp.float32
bf16 = jnp.bfloat16

D = 1024
DEPTH = 4
GRID_W = 64
N_MOD = 6
HEAD_DIM = 64
A_HEADS = 16
A_KV_HEADS = 4
WINDOW = 128
B_HEADS = 8
B_V_DIM = 128
N_EXPERTS = 8
ROPE_THETA = 10000.0
LN_EPS = 1e-5
RMS_EPS = 1e-5
NEG_INF = -1e30
ALPHA = (2.0 * DEPTH) ** 0.25
LOG2E = 1.4426950408889634

LANES = 128
SUB = 256
TM = 768
NSUB = TM // SUB
QB = 128
KB = 128
QB_B = 256
VMEM_LIMIT = 56 * 1024 * 1024

SH_A, SC_A, G_A, SH_F, SC_F, G_F = range(6)


def _dot(a, b):
    return jnp.dot(a, b, preferred_element_type=f32)


def _params(n_grid):
    return pltpu.CompilerParams(dimension_semantics=("arbitrary",) * n_grid,
                                vmem_limit_bytes=VMEM_LIMIT)


MOD_ROWS = 24
MOD_NB = 1536


def _mod_kernel(c_ref, w_ref, b_ref, o_ref):
    cond = c_ref[...]
    s = cond * (1.0 / (1.0 + jnp.exp(-cond)))
    o_ref[...] = _dot(s.astype(bf16), w_ref[...].astype(bf16)) + b_ref[...]


def _modulation(cond, w_mod, b_mod):
    n = N_MOD * D
    return pl.pallas_call(
        _mod_kernel,
        grid=(DEPTH, n // MOD_NB),
        in_specs=[pl.BlockSpec((MOD_ROWS, D), lambda i, j: (0, 0)),
                  pl.BlockSpec((None, D, MOD_NB), lambda i, j: (i, 0, j)),
                  pl.BlockSpec((None, 1, MOD_NB), lambda i, j: (i, 0, j))],
        out_specs=pl.BlockSpec((None, MOD_ROWS, MOD_NB), lambda i, j: (i, 0, j)),
        out_shape=jax.ShapeDtypeStruct((DEPTH, MOD_ROWS, n), f32),
        compiler_params=_params(2),
        name="modulation",
    )(cond, w_mod, b_mod.reshape(DEPTH, 1, n))


def _modulated(x_ref, mod_ref, sh, sc, dtype):
    hs = []
    for s in range(NSUB):
        xs = x_ref[s * SUB:(s + 1) * SUB, :]
        hs.append((xs * (1.0 + mod_ref[s, sc:sc + 1, :]) + mod_ref[s, sh:sh + 1, :]).astype(dtype))
    return jnp.concatenate(hs, axis=0)


def _qkv_kernel(x_ref, mod_ref, cos_ref, sa_ref, sb_ref, wq_ref, wk_ref, wv_ref,
                q_ref, kt_ref, v_ref, *, qscale):
    h = _modulated(x_ref, mod_ref, SH_A, SC_A, bf16)
    cos = cos_ref[...]
    sa = sa_ref[...]
    sb = sb_ref[...]

    def rope(y):
        return (y * cos + pltpu.roll(y, LANES - 16, 1) * sa + pltpu.roll(y, 16, 1) * sb)

    yq = _dot(h, wq_ref[...])
    for j in range(yq.shape[1] // LANES):
        blk = slice(j * LANES, (j + 1) * LANES)
        q_ref[:, blk] = (rope(yq[:, blk]) * qscale).astype(bf16)
    yk = _dot(h, wk_ref[...])
    for j in range(yk.shape[1] // LANES):
        kr = rope(yk[:, j * LANES:(j + 1) * LANES])
        for r in range(TM // KB):
            kt_ref[j, r] = kr[r * KB:(r + 1) * KB, :].T.astype(bf16)
    v_ref[...] = _dot(h, wv_ref[...]).astype(bf16)


def _qkv(X, modtab, rope_tabs, wq, wk, wv, qscale):
    B, T, _ = X.shape
    nq, nk, nv = wq.shape[1], wk.shape[1], wv.shape[1]
    cos, sa, sb = rope_tabs
    tab_spec = pl.BlockSpec((TM, LANES), lambda b, t: (t, 0))
    return pl.pallas_call(
        functools.partial(_qkv_kernel, qscale=qscale),
        grid=(B, T // TM),
        in_specs=[pl.BlockSpec((None, TM, D), lambda b, t: (b, t, 0)),
                  pl.BlockSpec((None, NSUB, N_MOD, D), lambda b, t: (b, t, 0, 0)),
                  tab_spec, tab_spec, tab_spec,
                  pl.BlockSpec((D, nq), lambda b, t: (0, 0)),
                  pl.BlockSpec((D, nk), lambda b, t: (0, 0)),
                  pl.BlockSpec((D, nv), lambda b, t: (0, 0))],
        out_specs=[pl.BlockSpec((None, TM, nq), lambda b, t: (b, t, 0)),
                   pl.BlockSpec((None, nk // LANES, TM // KB, LANES, KB), lambda b, t: (b, 0, t, 0, 0)),
                   pl.BlockSpec((None, TM, nv), lambda b, t: (b, t, 0))],
        out_shape=[jax.ShapeDtypeStruct((B, T, nq), bf16),
                   jax.ShapeDtypeStruct((B, nk // LANES, T // KB, LANES, KB), bf16),
                   jax.ShapeDtypeStruct((B, T, nv), bf16)],
        compiler_params=_params(2),
        name="qkv_proj",
    )(X, modtab, cos, sa, sb, wq, wk, wv)


A_PAIRS = 4


def _chain_update(state, q, kt, v, mask):
    s = _dot(q, kt)
    if mask is not None:
        s = jnp.where(mask, s, NEG_INF)
    if state is None:
        m_new = jnp.max(s, axis=1, keepdims=True)
        p = jnp.exp2(s - m_new)
        return m_new, jnp.sum(p, axis=1, keepdims=True), _dot(p.astype(bf16), v)
    m, l, acc = state
    m_new = jnp.maximum(m, jnp.max(s, axis=1, keepdims=True))
    alpha = jnp.exp2(m - m_new)
    p = jnp.exp2(s - m_new)
    return (m_new, alpha * l + jnp.sum(p, axis=1, keepdims=True),
            alpha * acc + _dot(p.astype(bf16), v))


def _half_masked(q):
    lo = lax.broadcasted_iota(jnp.int32, q.shape, 1) < HEAD_DIM
    zero = jnp.zeros_like(q)
    return [jnp.where(lo, q, zero), jnp.where(lo, zero, q)]


def _attn_a_kernel(sink_ref, q_ref, kt_ref, v_ref, o_ref, *, n_ctx, n_lat):
    hp = pl.program_id(1)
    n = pl.program_id(2)
    n_ctx_tiles = n_ctx // KB
    lo = lax.broadcasted_iota(jnp.int32, (QB, LANES), 1) < HEAD_DIM
    q = q_ref[...]
    chains = []
    for p in range(A_PAIRS):
        chains += _half_masked(q[:, p * LANES:(p + 1) * LANES])

    def init(i):
        return (jnp.full((QB, 1), sink_ref[hp * 2 * A_PAIRS + i], f32),
                jnp.ones((QB, 1), f32), jnp.zeros((QB, LANES), f32))

    def finish(states):
        for p in range(A_PAIRS):
            (_, l0, a0), (_, l1, a1) = states[2 * p], states[2 * p + 1]
            o_ref[:, p * LANES:(p + 1) * LANES] = jnp.where(lo, a0 * (1.0 / l0),
                                                          a1 * (1.0 / l1)).astype(bf16)

    ctx_kt = jnp.concatenate([kt_ref[j] for j in range(n_ctx_tiles)], axis=1)
    ctx_v = v_ref[0:n_ctx, :]

    @pl.when(n < n_ctx // QB)
    def _():
        finish([_chain_update(init(i), chains[i], ctx_kt, ctx_v, None)
                for i in range(len(chains))])

    @pl.when(n >= n_ctx // QB)
    def _():
        nl = n - n_ctx // QB
        first = jnp.clip(nl - 1, 0, n_lat // KB - 3)
        jb = n_ctx_tiles + first
        row0 = pl.multiple_of(n_ctx + first * KB, KB)
        kt01 = jnp.concatenate([kt_ref[jb], kt_ref[jb + 1]], axis=1)
        v01 = v_ref[pl.ds(row0, 2 * KB), :]
        kt2 = kt_ref[jb + 2]
        v2 = v_ref[pl.ds(row0 + 2 * KB, KB), :]
        rel = (first * KB + lax.broadcasted_iota(jnp.int32, (QB, 3 * KB), 1)
               - nl * QB - lax.broadcasted_iota(jnp.int32, (QB, 3 * KB), 0))
        ok = jnp.abs(rel) <= WINDOW
        states = []
        for i in range(len(chains)):
            st = _chain_update(init(i), chains[i], ctx_kt, ctx_v, None)
            st = _chain_update(st, chains[i], kt01, v01, ok[:, :2 * KB])
            st = _chain_update(st, chains[i], kt2, v2, ok[:, 2 * KB:])
            states.append(st)
        finish(states)


def _attn_a(q, kt, v, sink, n_ctx):
    B, T, nq = q.shape
    n_lat = T - n_ctx
    step_cols = A_PAIRS * LANES
    return pl.pallas_call(
        functools.partial(_attn_a_kernel, n_ctx=n_ctx, n_lat=n_lat),
        grid=(B, nq // step_cols, T // QB),
        in_specs=[pl.BlockSpec(memory_space=pltpu.SMEM),
                  pl.BlockSpec((None, QB, step_cols), lambda b, h, n: (b, n, h)),
                  pl.BlockSpec((None, None, T // KB, LANES, KB), lambda b, h, n: (b, h, 0, 0, 0)),
                  pl.BlockSpec((None, T, LANES), lambda b, h, n: (b, 0, h))],
        out_specs=pl.BlockSpec((None, QB, step_cols), lambda b, h, n: (b, n, h)),
        out_shape=jax.ShapeDtypeStruct((B, T, nq), bf16),
        compiler_params=_params(3),
        name="attn_window",
    )(sink, q, kt, v)


B_CHUNK_TILES = 2
B_STEP_HEADS = 4


def _attn_b_kernel(lam_ref, q_ref, kt_ref, v_ref, g_ref, o_ref, *, n_ctx, out_scale):
    n = pl.program_id(2)
    lam = lam_ref[0]
    n_tiles = kt_ref.shape[1]
    q = q_ref[...]
    units = [(h, r) for h in range(B_STEP_HEADS) for r in range(QB_B // QB)]
    chains = []
    for h, r in units:
        chains += _half_masked(q[r * QB:(r + 1) * QB, h * LANES:(h + 1) * LANES])

    def run(n_key_tiles):
        states = [None] * len(chains)
        for t0 in range(0, n_key_tiles, B_CHUNK_TILES):
            t1 = min(t0 + B_CHUNK_TILES, n_key_tiles)
            for u, (h, r) in enumerate(units):
                kt = jnp.concatenate([kt_ref[h, j] for j in range(t0, t1)], axis=1)
                v = v_ref[t0 * KB:t1 * KB, h * LANES:(h + 1) * LANES]
                for i in (2 * u, 2 * u + 1):
                    states[i] = _chain_update(states[i], chains[i], kt, v, None)
        for u, (h, r) in enumerate(units):
            (_, l1, a1), (_, l2, a2) = states[2 * u], states[2 * u + 1]
            od = a1 * (1.0 / l1) - lam * (a2 * (1.0 / l2))
            y = od * lax.rsqrt(jnp.mean(od * od, axis=1, keepdims=True) + RMS_EPS)
            o_ref[r * QB:(r + 1) * QB, h * LANES:(h + 1) * LANES] = (
                y * g_ref[...] * out_scale).astype(bf16)

    @pl.when(n < n_ctx // QB_B)
    def _():
        run(n_ctx // KB)

    @pl.when(n >= n_ctx // QB_B)
    def _():
        run(n_tiles)


def _attn_b(q, kt, v, lam, subln_g, n_ctx, out_scale):
    B, T, nq = q.shape
    cols = B_STEP_HEADS * LANES
    return pl.pallas_call(
        functools.partial(_attn_b_kernel, n_ctx=n_ctx, out_scale=out_scale),
        grid=(B, nq // cols, T // QB_B),
        in_specs=[pl.BlockSpec(memory_space=pltpu.SMEM),
                  pl.BlockSpec((None, QB_B, cols), lambda b, h, n: (b, n, h)),
                  pl.BlockSpec((None, B_STEP_HEADS, T // KB, LANES, KB), lambda b, h, n: (b, h, 0, 0, 0)),
                  pl.BlockSpec((None, T, cols), lambda b, h, n: (b, 0, h)),
                  pl.BlockSpec((1, LANES), lambda b, h, n: (0, 0))],
        out_specs=pl.BlockSpec((None, QB_B, cols), lambda b, h, n: (b, n, h)),
        out_shape=jax.ShapeDtypeStruct((B, T, nq), bf16),
        compiler_params=_params(3),
        name="attn_diff",
    )(lam, q, kt, v, subln_g)


def _residual_ln(x, y, gate, ln_g, ln_b):
    z = ALPHA * x + (1.0 + gate) * y
    mu = jnp.mean(z, axis=1, keepdims=True)
    zc = z - mu
    var = jnp.mean(zc * zc, axis=1, keepdims=True)
    return zc * lax.rsqrt(var + LN_EPS) * ln_g + ln_b


def _oproj_kernel(o_ref, x_ref, mod_ref, w_ref, g_ref, b_ref, xo_ref):
    y = _dot(o_ref[...], w_ref[...])
    for s in range(NSUB):
        rows = slice(s * SUB, (s + 1) * SUB)
        xo_ref[rows, :] = _residual_ln(x_ref[rows, :], y[rows, :], mod_ref[s, G_A:G_A + 1, :],
                                       g_ref[...], b_ref[...])


def _oproj(o, X, modtab, w_o, ln_g, ln_b):
    B, T, _ = X.shape
    row_spec = pl.BlockSpec((None, TM, D), lambda b, t: (b, t, 0))
    vec_spec = pl.BlockSpec((1, D), lambda b, t: (0, 0))
    return pl.pallas_call(
        _oproj_kernel,
        grid=(B, T // TM),
        in_specs=[row_spec, row_spec,
                  pl.BlockSpec((None, NSUB, N_MOD, D), lambda b, t: (b, t, 0, 0)),
                  pl.BlockSpec((D, D), lambda b, t: (0, 0)),
                  vec_spec, vec_spec],
        out_specs=row_spec,
        out_shape=jax.ShapeDtypeStruct((B, T, D), f32),
        compiler_params=_params(2),
        name="oproj_ln",
    )(o, X, modtab, w_o, ln_g, ln_b)


FFN_FC = 1408


def _ffn_kernel(x_ref, mod_ref, wg_ref, wu_ref, wd_ref, lng_ref, lnb_ref, o_ref, h_scr, acc_scr):
    f = pl.program_id(2)

    @pl.when(f == 0)
    def _():
        h_scr[...] = _modulated(x_ref, mod_ref, SH_F, SC_F, bf16)
        acc_scr[...] = jnp.zeros_like(acc_scr)

    h = h_scr[...]
    g = _dot(h, wg_ref[...])
    u = _dot(h, wu_ref[...])
    a = g * (1.0 / (1.0 + jnp.exp(-g))) * u
    acc_scr[...] += _dot(a.astype(bf16), wd_ref[...])

    @pl.when(f == pl.num_programs(2) - 1)
    def _():
        for s in range(NSUB):
            rows = slice(s * SUB, (s + 1) * SUB)
            o_ref[rows, :] = _residual_ln(x_ref[rows, :], acc_scr[rows, :],
                                          mod_ref[s, G_F:G_F + 1, :], lng_ref[...], lnb_ref[...])


def _ffn(X, modtab, wg, wu, wd, ln_g, ln_b):
    B, T, _ = X.shape
    n_hidden = wg.shape[1]
    row_spec = pl.BlockSpec((None, TM, D), lambda b, t, f: (b, t, 0))
    vec_spec = pl.BlockSpec((1, D), lambda b, t, f: (0, 0))
    return pl.pallas_call(
        _ffn_kernel,
        grid=(B, T // TM, n_hidden // FFN_FC),
        in_specs=[row_spec, pl.BlockSpec((None, NSUB, N_MOD, D), lambda b, t, f: (b, t, 0, 0)),
                  pl.BlockSpec((D, FFN_FC), lambda b, t, f: (0, f)),
                  pl.BlockSpec((D, FFN_FC), lambda b, t, f: (0, f)),
                  pl.BlockSpec((FFN_FC, D), lambda b, t, f: (f, 0)),
                  vec_spec, vec_spec],
        out_specs=row_spec,
        out_shape=jax.ShapeDtypeStruct((B, T, D), f32),
        scratch_shapes=[pltpu.VMEM((TM, D), bf16), pltpu.VMEM((TM, D), f32)],
        compiler_params=_params(3),
        name="ffn_dense",
    )(X, modtab, wg, wu, wd, ln_g, ln_b)


def _router_kernel(x_ref, mod_ref, wr_ref, tri_ref, h_ref, gates_ref, slots_ref, cnt_ref, carry_scr):
    t = pl.program_id(1)

    @pl.when(t == 0)
    def _():
        carry_scr[...] = jnp.zeros_like(carry_scr)

    lane = lax.broadcasted_iota(jnp.int32, (SUB, LANES), 1)
    starts = []
    for s in range(NSUB):
        rows = slice(s * SUB, (s + 1) * SUB)
        h = x_ref[rows, :] * (1.0 + mod_ref[s, SC_F:SC_F + 1, :]) + mod_ref[s, SH_F:SH_F + 1, :]
        h_ref[rows, :] = h.astype(bf16)
        logits = jnp.dot(h, wr_ref[...], preferred_element_type=f32,
                         precision=lax.Precision.HIGHEST)
        lg = jnp.where(lane < N_EXPERTS, logits, -jnp.inf)
        v1 = jnp.max(lg, axis=1, keepdims=True)
        i1 = jnp.min(jnp.where(lg == v1, lane, LANES), axis=1, keepdims=True)
        lg2 = jnp.where(lane == i1, -jnp.inf, lg)
        v2 = jnp.max(lg2, axis=1, keepdims=True)
        i2 = jnp.min(jnp.where(lg2 == v2, lane, LANES), axis=1, keepdims=True)
        ex = jnp.exp(v2 - v1)
        w1 = 1.0 / (1.0 + ex)
        w2 = ex / (1.0 + ex)
        sel = (lane == i1) | (lane == i2)
        gates_ref[rows, :] = jnp.where(lane == i1, w1, jnp.where(lane == i2, w2, 0.0))
        cum = _dot(tri_ref[...], jnp.where(sel, 1.0, 0.0).astype(bf16))
        carry = carry_scr[...]
        starts.append(carry)
        slots_ref[rows, :] = jnp.where(sel, carry + cum - 1.0, -1.0)
        carry_scr[...] = carry + cum[SUB - 1:SUB, :]
    pad = jnp.zeros((8 - NSUB - 1, LANES), f32)
    cnt_ref[...] = jnp.concatenate(starts + [carry_scr[...], pad], axis=0)


def _router(X, modtab, w_router_padded, tri):
    B, T, _ = X.shape
    nt = T // TM
    row128 = pl.BlockSpec((None, TM, LANES), lambda b, t: (b, t, 0))
    return pl.pallas_call(
        _router_kernel,
        grid=(B, nt),
        in_specs=[pl.BlockSpec((None, TM, D), lambda b, t: (b, t, 0)),
                  pl.BlockSpec((None, NSUB, N_MOD, D), lambda b, t: (b, t, 0, 0)),
                  pl.BlockSpec((D, LANES), lambda b, t: (0, 0)),
                  pl.BlockSpec((SUB, SUB), lambda b, t: (0, 0))],
        out_specs=[pl.BlockSpec((None, TM, D), lambda b, t: (b, t, 0)), row128, row128,
                   pl.BlockSpec((None, None, 8, LANES), lambda b, t: (b, t, 0, 0))],
        out_shape=[jax.ShapeDtypeStruct((B, T, D), bf16),
                   jax.ShapeDtypeStruct((B, T, LANES), f32),
                   jax.ShapeDtypeStruct((B, T, LANES), f32),
                   jax.ShapeDtypeStruct((B, nt, 8, LANES), f32)],
        scratch_shapes=[pltpu.VMEM((1, LANES), f32)],
        compiler_params=_params(2),
        name="router",
    )(X, modtab, w_router_padded, tri)


CH = 128
MOE_FC = 1792
GATHER_GROUPS = 3
SCATTER_CHUNKS = 2


def _moe_kernel(nch_ref, klo_ref, khi_ref, clo_ref, chi_ref,
                h_ref, slots_ref, slots_t_ref, gates_t_ref, wg_ref, wu_ref, wd_ref,
                y_ref, xs_scr, ys_scr, gate_scr, *, max_chunks, n_groups, n_f):
    b = pl.program_id(0)
    e = pl.program_id(1)
    f = pl.program_id(2)
    be = b * pl.num_programs(1) + e
    nch = nch_ref[be]

    @pl.when((b == 0) & (e == 0) & (f == 0))
    def _():
        xs_scr[...] = jnp.zeros_like(xs_scr)
        ys_scr[...] = jnp.zeros_like(ys_scr)
        gate_scr[...] = jnp.zeros_like(gate_scr)

    @pl.when((e == 0) & (f == 0))
    def _():
        y_ref[...] = jnp.zeros_like(y_ref)

    @pl.when(f == 0)
    def _gather():
        def chunk(c, carry):
            rows = pl.ds(pl.multiple_of(c * CH, CH), CH)
            rank = (lax.broadcasted_iota(jnp.int32, (CH, SUB), 0) + c * CH).astype(f32)
            lo = klo_ref[be * max_chunks + c]
            hi = khi_ref[be * max_chunks + c]
            g0 = jnp.minimum(lo, n_groups - GATHER_GROUPS)
            hits = [slots_t_ref[e, pl.ds(g0 + j, 1), :] == rank for j in range(GATHER_GROUPS)]
            p = jnp.concatenate([jnp.where(hit, 1.0, 0.0).astype(bf16) for hit in hits], axis=1)
            window = h_ref[pl.ds(pl.multiple_of(g0 * SUB, SUB), GATHER_GROUPS * SUB), :]
            xs_scr[rows, :] = _dot(p, window).astype(bf16)
            gate = jnp.zeros((CH, 1), f32)
            for j, hit in enumerate(hits):
                gate = gate + jnp.sum(jnp.where(hit, gates_t_ref[e, pl.ds(g0 + j, 1), :], 0.0),
                                      axis=1, keepdims=True)
            gate_scr[rows, :] = gate

            @pl.when(hi >= g0 + GATHER_GROUPS)
            def _():
                def group(kb, acc):
                    xacc, gacc = acc
                    hit = slots_t_ref[e, pl.ds(kb, 1), :] == rank
                    xacc = xacc + _dot(jnp.where(hit, 1.0, 0.0).astype(bf16),
                                       h_ref[pl.ds(pl.multiple_of(kb * SUB, SUB), SUB), :])
                    gacc = gacc + jnp.sum(jnp.where(hit, gates_t_ref[e, pl.ds(kb, 1), :], 0.0),
                                          axis=1, keepdims=True)
                    return xacc, gacc

                xacc, gacc = lax.fori_loop(lo, hi + 1, group,
                                           (jnp.zeros((CH, D), f32), jnp.zeros((CH, 1), f32)))
                xs_scr[rows, :] = xacc.astype(bf16)
                gate_scr[rows, :] = gacc

            return carry

        lax.fori_loop(0, nch, chunk, 0)

    def expert_rows(r0, m, stage):
        rows = pl.ds(r0, m)
        xs = xs_scr[rows, :]
        g = _dot(xs, wg_ref[...])
        u = _dot(xs, wu_ref[...])
        a = g * (1.0 / (1.0 + jnp.exp(-g))) * u * gate_scr[rows, :]
        d = _dot(a.astype(bf16), wd_ref[...])
        if stage == "first":
            ys_scr[rows, :] = d
        elif stage == "middle":
            ys_scr[rows, :] += d
        else:
            xs_scr[rows, :] = (ys_scr[rows, :] + d).astype(bf16)

    def run_experts(stage):
        def pair(i, carry):
            expert_rows(pl.multiple_of(i * 2 * CH, 2 * CH), 2 * CH, stage)
            return carry

        lax.fori_loop(0, nch // 2, pair, 0)

        @pl.when(nch % 2 == 1)
        def _():
            expert_rows(pl.multiple_of((nch - 1) * CH, CH), CH, stage)

    @pl.when(f == 0)
    def _():
        run_experts("first")

    if n_f > 2:
        @pl.when((f > 0) & (f < n_f - 1))
        def _():
            run_experts("middle")

    @pl.when(f == n_f - 1)
    def _scatter():
        run_experts("last")
        lane_e = lax.broadcasted_iota(jnp.int32, (SUB, LANES), 1)
        lane_w = lax.broadcasted_iota(jnp.int32, (SUB, SCATTER_CHUNKS * CH), 1)
        lane_c = lax.broadcasted_iota(jnp.int32, (SUB, CH), 1)

        def expert_col(rows):
            return jnp.sum(jnp.where(lane_e == e, slots_ref[rows, :], 0.0), axis=1, keepdims=True)

        for kb in range(n_groups):
            rows = slice(kb * SUB, (kb + 1) * SUB)
            c0 = jnp.minimum(clo_ref[be * n_groups + kb], max_chunks - SCATTER_CHUNKS)
            pt = jnp.where(expert_col(rows) == (lane_w + c0 * CH).astype(f32), 1.0, 0.0).astype(bf16)
            window = xs_scr[pl.ds(pl.multiple_of(c0 * CH, CH), SCATTER_CHUNKS * CH), :]
            y_ref[rows, :] += _dot(pt, window)

        def spill(kb, carry):
            rows = pl.ds(pl.multiple_of(kb * SUB, SUB), SUB)
            c0 = jnp.minimum(clo_ref[be * n_groups + kb], max_chunks - SCATTER_CHUNKS)

            def chunk(c, carry2):
                pt = jnp.where(expert_col(rows) == (lane_c + c * CH).astype(f32), 1.0, 0.0).astype(bf16)
                y_ref[rows, :] += _dot(pt, xs_scr[pl.ds(pl.multiple_of(c * CH, CH), CH), :])
                return carry2

            lax.fori_loop(c0 + SCATTER_CHUNKS, chi_ref[be * n_groups + kb] + 1, chunk, 0)
            return carry

        lax.fori_loop(0, n_groups, spill, 0)


def _moe(h, slots, slots_t, gates_t, tables, wg, wu, wd):
    B, T, _ = h.shape
    n_e, _, n_hidden = wg.shape
    max_chunks = T // CH
    n_groups = T // SUB
    n_f = n_hidden // MOE_FC
    assert n_f >= 2 and n_f * MOE_FC == n_hidden
    once = pl.Buffered(1)
    grid_spec = pltpu.PrefetchScalarGridSpec(
        num_scalar_prefetch=len(tables),
        grid=(B, n_e, n_f),
        in_specs=[pl.BlockSpec((None, T, D), lambda b, e, f, *_: (b, 0, 0), pipeline_mode=once),
                  pl.BlockSpec((None, T, LANES), lambda b, e, f, *_: (b, 0, 0), pipeline_mode=once),
                  pl.BlockSpec((None, n_e, n_groups, SUB), lambda b, e, f, *_: (b, 0, 0, 0)),
                  pl.BlockSpec((None, n_e, n_groups, SUB), lambda b, e, f, *_: (b, 0, 0, 0)),
                  pl.BlockSpec((None, D, MOE_FC), lambda b, e, f, *_: (e, 0, f)),
                  pl.BlockSpec((None, D, MOE_FC), lambda b, e, f, *_: (e, 0, f)),
                  pl.BlockSpec((None, MOE_FC, D), lambda b, e, f, *_: (e, f, 0))],
        out_specs=pl.BlockSpec((None, T, D), lambda b, e, f, *_: (b, 0, 0), pipeline_mode=once),
        scratch_shapes=[pltpu.VMEM((T, D), bf16), pltpu.VMEM((T, D), f32),
                        pltpu.VMEM((T, 1), f32)])
    return pl.pallas_call(
        functools.partial(_moe_kernel, max_chunks=max_chunks, n_groups=n_groups, n_f=n_f),
        grid_spec=grid_spec,
        out_shape=jax.ShapeDtypeStruct((B, T, D), f32),
        compiler_params=_params(3),
        name="moe_routed",
    )(*tables, h, slots, slots_t, gates_t, wg, wu, wd)


def _routing_tables(slots, gates, cnt):
    B, T, _ = slots.shape
    n_groups = T // SUB
    max_chunks = T // CH

    def expert_major(a):
        return a[:, :, :N_EXPERTS].reshape(B, n_groups, SUB, N_EXPERTS).transpose(0, 3, 1, 2)

    starts = cnt[:, :, :NSUB, :N_EXPERTS].reshape(B, n_groups, N_EXPERTS).astype(jnp.int32)
    starts = starts.transpose(0, 2, 1)
    total = cnt[:, -1, NSUB, :N_EXPERTS].astype(jnp.int32)
    nch = (total + CH - 1) // CH
    r0 = jnp.arange(max_chunks, dtype=jnp.int32)[None, None, :] * CH
    r1 = jnp.minimum(total[:, :, None], r0 + CH) - 1
    st = starts[:, :, None, :]
    klo = jnp.sum(st <= r0[..., None], axis=-1) - 1
    khi = jnp.sum(st <= r1[..., None], axis=-1) - 1
    live = r0 < total[:, :, None]
    klo = jnp.where(live, klo, 0)
    khi = jnp.where(live, khi, -1)
    ends = jnp.concatenate([starts[:, :, 1:], total[:, :, None]], axis=-1)
    clo = starts // CH
    chi = jnp.where(ends > starts, (ends - 1) // CH, clo - 1)
    tables = tuple(a.astype(jnp.int32).reshape(-1) for a in (nch, klo, khi, clo, chi))
    return expert_major(slots), expert_major(gates), tables


def _resid_ln_kernel(x_ref, y_ref, mod_ref, g_ref, b_ref, o_ref):
    for s in range(NSUB):
        rows = slice(s * SUB, (s + 1) * SUB)
        o_ref[rows, :] = _residual_ln(x_ref[rows, :], y_ref[rows, :], mod_ref[s, G_F:G_F + 1, :],
                                      g_ref[...], b_ref[...])


def _resid_ln(X, y, modtab, ln_g, ln_b):
    B, T, _ = X.shape
    row_spec = pl.BlockSpec((None, TM, D), lambda b, t: (b, t, 0))
    vec_spec = pl.BlockSpec((1, D), lambda b, t: (0, 0))
    return pl.pallas_call(
        _resid_ln_kernel,
        grid=(B, T // TM),
        in_specs=[row_spec, row_spec,
                  pl.BlockSpec((None, NSUB, N_MOD, D), lambda b, t: (b, t, 0, 0)),
                  vec_spec, vec_spec],
        out_specs=row_spec,
        out_shape=jax.ShapeDtypeStruct((B, T, D), f32),
        compiler_params=_params(2),
        name="resid_ln",
    )(X, y, modtab, ln_g, ln_b)


def _rope_tables(n_ctx, n_lat):
    rows = n_lat // GRID_W
    row = jnp.repeat(jnp.arange(rows, dtype=f32), GRID_W)
    col = jnp.tile(jnp.arange(GRID_W, dtype=f32), rows)
    half = HEAD_DIM // 2
    inv = ROPE_THETA ** (-jnp.arange(0, half, 2, dtype=f32) / half)
    ang_r = row[:, None] * inv[None, :]
    ang_c = col[:, None] * inv[None, :]
    ang = jnp.concatenate([ang_r, ang_r, ang_c, ang_c], axis=-1)
    cos, sin = jnp.cos(ang), jnp.sin(ang)
    quarter = (np.arange(HEAD_DIM) // (HEAD_DIM // 4)) % 2
    sa = jnp.where(quarter == 0, -sin, 0.0)
    sb = jnp.where(quarter == 1, sin, 0.0)
    reps = LANES // HEAD_DIM

    def full(tab, ctx_value):
        tab = jnp.concatenate([jnp.full((n_ctx, HEAD_DIM), ctx_value, f32), tab], axis=0)
        return jnp.tile(tab, (1, reps))

    return full(cos, 1.0), full(sa, 0.0), full(sb, 0.0)


def _a_head_perm():
    group = A_HEADS // A_KV_HEADS
    order = []
    for hp in range(A_KV_HEADS // 2):
        for j in range(group):
            order += [(2 * hp) * group + j, (2 * hp + 1) * group + j]
    return np.asarray(order)


def kernel(x, c, ctx, c_ctx, w_mod, b_mod, ln_g, ln_b, a_w_qkv, a_sink, a_w_o,
           b_w_qkv, b_lam_q1, b_lam_k1, b_lam_q2, b_lam_k2, b_subln_g, b_w_o,
           ff_w_gate, ff_w_up, ff_w_down, moe_w_router, moe_w_gate, moe_w_up, moe_w_down):
    B, L, _ = x.shape
    C = ctx.shape[1]
    T = C + L
    X = jnp.concatenate([ctx, x], axis=1)

    cond = jnp.concatenate([c, c_ctx[None, :], jnp.zeros((MOD_ROWS - B - 1, D), f32)], axis=0)
    mods = _modulation(cond, w_mod, b_mod).reshape(DEPTH, MOD_ROWS, N_MOD, D)
    rope_tabs = _rope_tables(C, L)
    tri = jnp.asarray(np.tril(np.ones((SUB, SUB), np.float32)), bf16)
    perm = _a_head_perm()
    perm_cols = (perm[:, None] * HEAD_DIM + np.arange(HEAD_DIM)[None, :]).reshape(-1)
    qscale = HEAD_DIM ** -0.5 * LOG2E
    a_qd = A_HEADS * HEAD_DIM
    a_kd = A_KV_HEADS * HEAD_DIM
    b_qd = 2 * B_HEADS * HEAD_DIM

    for i in range(DEPTH):
        j = i // 2
        m_lat = jnp.broadcast_to(mods[i, :B, None], (B, L // SUB, N_MOD, D))
        m_ctx = jnp.broadcast_to(mods[i, B][None, None], (B, C // SUB, N_MOD, D))
        modtab = jnp.concatenate([m_ctx, m_lat], axis=1)

        if i % 2 == 0:
            w = a_w_qkv[j]
            wq = w[:, :a_qd][:, perm_cols].astype(bf16)
            wk = w[:, a_qd:a_qd + a_kd].astype(bf16)
            wv = w[:, a_qd + a_kd:].astype(bf16)
            q, k, v = _qkv(X, modtab, rope_tabs, wq, wk, wv, qscale)
            sink = a_sink[j][perm].astype(f32) * LOG2E
            o = _attn_a(q, k, v, sink, C)
            w_o = a_w_o[j][perm_cols, :].astype(bf16)
        else:
            w = b_w_qkv[j]
            wq = w[:, :b_qd].astype(bf16)
            wk = w[:, b_qd:2 * b_qd].astype(bf16)
            wv = w[:, 2 * b_qd:].astype(bf16)
            q, k, v = _qkv(X, modtab, rope_tabs, wq, wk, wv, qscale)
            lam_init = 0.8 - 0.6 * math.exp(-0.3 * i)
            lam = (jnp.exp(jnp.sum(b_lam_q1[j] * b_lam_k1[j]))
                   - jnp.exp(jnp.sum(b_lam_q2[j] * b_lam_k2[j])) + lam_init).reshape(1)
            o = _attn_b(q, k, v, lam, b_subln_g[j].reshape(1, B_V_DIM), C, 1.0 - lam_init)
            w_o = b_w_o[j].astype(bf16)
        X = _oproj(o, X, modtab, w_o, ln_g[i, 0].reshape(1, D), ln_b[i, 0].reshape(1, D))

        lng = ln_g[i, 1].reshape(1, D)
        lnb = ln_b[i, 1].reshape(1, D)
        if i % 2 == 0:
            X = _ffn(X, modtab, ff_w_gate[j].astype(bf16), ff_w_up[j].astype(bf16),
                     ff_w_down[j].astype(bf16), lng, lnb)
        else:
            wr = jnp.pad(moe_w_router[j], ((0, 0), (0, LANES - N_EXPERTS)))
            h, gates, slots, cnt = _router(X, modtab, wr, tri)
            slots_t, gates_t, tables = _routing_tables(slots, gates, cnt)
            y = _moe(h, slots, slots_t, gates_t, tables, moe_w_gate[j].astype(bf16),
                     moe_w_up[j].astype(bf16), moe_w_down[j].astype(bf16))
            X = _resid_ln(X, y, modtab, lng, lnb)
    return X[:, C:, :]
```

```python
import functools
import math

import numpy as np
import jax
import jax.numpy as jnp
from jax import lax
from jax.experimental import pallas as pl
from jax.experimental.pallas import tpu as pltpu

f32 = jnp.float32
bf16 = jnp.bfloat16

D = 1024
DEPTH = 4
GRID_W = 64
N_MOD = 6
HEAD_DIM = 64
A_HEADS = 16
A_KV_HEADS = 4
WINDOW = 128
B_HEADS = 8
B_V_DIM = 128
N_EXPERTS = 8
ROPE_THETA = 10000.0
LN_EPS = 1e-5
RMS_EPS = 1e-5
NEG_INF = -1e30
ALPHA = (2.0 * DEPTH) ** 0.25
LOG2E = 1.4426950408889634

LANES = 128
SUB = 256
TM = 768
NSUB = TM // SUB
TM_LATENT = 1024
QB = 128
KB = 128
QB_B = 256
VMEM_LIMIT = 56 * 1024 * 1024

SH_A, SC_A, G_A, SH_F, SC_F, G_F = range(6)


def _dot(a, b):
    return jnp.dot(a, b, preferred_element_type=f32)


def _params(n_grid):
    return pltpu.CompilerParams(dimension_semantics=("arbitrary",) * n_grid,
                                vmem_limit_bytes=VMEM_LIMIT)


MOD_ROWS = 24
MOD_NB = 1536


def _mod_kernel(c_ref, w_ref, b_ref, o_ref):
    cond = c_ref[...]
    s = cond * (1.0 / (1.0 + jnp.exp(-cond)))
    o_ref[...] = _dot(s.astype(bf16), w_ref[...].astype(bf16)) + b_ref[...]


def _modulation(cond, w_mod, b_mod):
    n = N_MOD * D
    return pl.pallas_call(
        _mod_kernel,
        grid=(DEPTH, n // MOD_NB),
        in_specs=[pl.BlockSpec((MOD_ROWS, D), lambda i, j: (0, 0)),
                  pl.BlockSpec((None, D, MOD_NB), lambda i, j: (i, 0, j)),
                  pl.BlockSpec((None, 1, MOD_NB), lambda i, j: (i, 0, j))],
        out_specs=pl.BlockSpec((None, MOD_ROWS, MOD_NB), lambda i, j: (i, 0, j)),
        out_shape=jax.ShapeDtypeStruct((DEPTH, MOD_ROWS, n), f32),
        compiler_params=_params(2),
        name="modulation",
    )(cond, w_mod, b_mod.reshape(DEPTH, 1, n))


def _modulated(x_ref, mod_ref, s, sh, sc):
    x = x_ref[s * SUB:(s + 1) * SUB, :]
    return x * (1.0 + mod_ref[s, sc:sc + 1, :]) + mod_ref[s, sh:sh + 1, :]


def _qkv_kernel(x_ref, mod_ref, cos_ref, sa_ref, sb_ref, wq_ref, wk_ref, wv_ref,
                q_ref, kt_ref, v_ref, *, qscale):
    for s in range(x_ref.shape[0] // SUB):
        rows = slice(s * SUB, (s + 1) * SUB)
        h = _modulated(x_ref, mod_ref, s, SH_A, SC_A).astype(bf16)
        cos = cos_ref[rows, :]
        sa = sa_ref[rows, :]
        sb = sb_ref[rows, :]

        def rope(y):
            return (y * cos + pltpu.roll(y, LANES - 16, 1) * sa + pltpu.roll(y, 16, 1) * sb)

        yq = _dot(h, wq_ref[...])
        for j in range(yq.shape[1] // LANES):
            blk = slice(j * LANES, (j + 1) * LANES)
            q_ref[rows, blk] = (rope(yq[:, blk]) * qscale).astype(bf16)
        yk = _dot(h, wk_ref[...])
        for j in range(yk.shape[1] // LANES):
            kr = rope(yk[:, j * LANES:(j + 1) * LANES])
            for r in range(SUB // KB):
                kt_ref[j, s * (SUB // KB) + r] = kr[r * KB:(r + 1) * KB, :].T.astype(bf16)
        v_ref[rows, :] = _dot(h, wv_ref[...]).astype(bf16)


def _qkv(X, modtab, rope_tabs, wq, wk, wv, qscale):
    B, T, _ = X.shape
    nq, nk, nv = wq.shape[1], wk.shape[1], wv.shape[1]
    cos, sa, sb = rope_tabs
    tab_spec = pl.BlockSpec((TM, LANES), lambda b, t: (t, 0))
    return pl.pallas_call(
        functools.partial(_qkv_kernel, qscale=qscale),
        grid=(B, T // TM),
        in_specs=[pl.BlockSpec((None, TM, D), lambda b, t: (b, t, 0)),
                  pl.BlockSpec((None, NSUB, N_MOD, D), lambda b, t: (b, t, 0, 0)),
                  tab_spec, tab_spec, tab_spec,
                  pl.BlockSpec((D, nq), lambda b, t: (0, 0)),
                  pl.BlockSpec((D, nk), lambda b, t: (0, 0)),
                  pl.BlockSpec((D, nv), lambda b, t: (0, 0))],
        out_specs=[pl.BlockSpec((None, TM, nq), lambda b, t: (b, t, 0)),
                   pl.BlockSpec((None, nk // LANES, TM // KB, LANES, KB), lambda b, t: (b, 0, t, 0, 0)),
                   pl.BlockSpec((None, TM, nv), lambda b, t: (b, t, 0))],
        out_shape=[jax.ShapeDtypeStruct((B, T, nq), bf16),
                   jax.ShapeDtypeStruct((B, nk // LANES, T // KB, LANES, KB), bf16),
                   jax.ShapeDtypeStruct((B, T, nv), bf16)],
        compiler_params=_params(2),
        name="qkv_proj",
    )(X, modtab, cos, sa, sb, wq, wk, wv)


A_PAIRS = 4


def _chain_update(state, q, kt, v, mask):
    s = _dot(q, kt)
    if mask is not None:
        s = jnp.where(mask, s, NEG_INF)
    if state is None:
        m_new = jnp.max(s, axis=1, keepdims=True)
        p = jnp.exp2(s - m_new)
        return m_new, jnp.sum(p, axis=1, keepdims=True), _dot(p.astype(bf16), v)
    m, l, acc = state
    m_new = jnp.maximum(m, jnp.max(s, axis=1, keepdims=True))
    alpha = jnp.exp2(m - m_new)
    p = jnp.exp2(s - m_new)
    return (m_new, alpha * l + jnp.sum(p, axis=1, keepdims=True),
            alpha * acc + _dot(p.astype(bf16), v))


def _half_masked(q):
    lo = lax.broadcasted_iota(jnp.int32, q.shape, 1) < HEAD_DIM
    zero = jnp.zeros_like(q)
    return [jnp.where(lo, q, zero), jnp.where(lo, zero, q)]


def _attn_a_kernel(sink_ref, q_ref, kt_ref, v_ref, o_ref, *, n_ctx, n_lat, q_block0):
    hp = pl.program_id(1)
    n = pl.program_id(2) + q_block0
    n_ctx_tiles = n_ctx // KB
    lo = lax.broadcasted_iota(jnp.int32, (QB, LANES), 1) < HEAD_DIM
    q = q_ref[...]
    chains = []
    for p in range(A_PAIRS):
        chains += _half_masked(q[:, p * LANES:(p + 1) * LANES])

    def init(i):
        return (jnp.full((QB, 1), sink_ref[hp * 2 * A_PAIRS + i], f32),
                jnp.ones((QB, 1), f32), jnp.zeros((QB, LANES), f32))

    def finish(states):
        for p in range(A_PAIRS):
            (_, l0, a0), (_, l1, a1) = states[2 * p], states[2 * p + 1]
            o_ref[:, p * LANES:(p + 1) * LANES] = jnp.where(lo, a0 * (1.0 / l0),
                                                          a1 * (1.0 / l1)).astype(bf16)

    ctx_kt = jnp.concatenate([kt_ref[j] for j in range(n_ctx_tiles)], axis=1)
    ctx_v = v_ref[0:n_ctx, :]

    @pl.when(n < n_ctx // QB)
    def _():
        finish([_chain_update(init(i), chains[i], ctx_kt, ctx_v, None)
                for i in range(len(chains))])

    @pl.when(n >= n_ctx // QB)
    def _():
        nl = n - n_ctx // QB
        first = jnp.clip(nl - 1, 0, n_lat // KB - 3)
        jb = n_ctx_tiles + first
        row0 = pl.multiple_of(n_ctx + first * KB, KB)
        kt01 = jnp.concatenate([kt_ref[jb], kt_ref[jb + 1]], axis=1)
        v01 = v_ref[pl.ds(row0, 2 * KB), :]
        kt2 = kt_ref[jb + 2]
        v2 = v_ref[pl.ds(row0 + 2 * KB, KB), :]
        rel = (first * KB + lax.broadcasted_iota(jnp.int32, (QB, 3 * KB), 1)
               - nl * QB - lax.broadcasted_iota(jnp.int32, (QB, 3 * KB), 0))
        ok = jnp.abs(rel) <= WINDOW
        states = []
        for i in range(len(chains)):
            st = _chain_update(init(i), chains[i], ctx_kt, ctx_v, None)
            st = _chain_update(st, chains[i], kt01, v01, ok[:, :2 * KB])
            st = _chain_update(st, chains[i], kt2, v2, ok[:, 2 * KB:])
            states.append(st)
        finish(states)


def _attn_a(q, kt, v, sink, n_ctx, q_row0):
    B, T, nq = q.shape
    n_lat = T - n_ctx
    step_cols = A_PAIRS * LANES
    q_block0 = q_row0 // QB
    return pl.pallas_call(
        functools.partial(_attn_a_kernel, n_ctx=n_ctx, n_lat=n_lat, q_block0=q_block0),
        grid=(B, nq // step_cols, T // QB - q_block0),
        in_specs=[pl.BlockSpec(memory_space=pltpu.SMEM),
                  pl.BlockSpec((None, QB, step_cols), lambda b, h, n: (b, n + q_block0, h)),
                  pl.BlockSpec((None, None, T // KB, LANES, KB), lambda b, h, n: (b, h, 0, 0, 0)),
                  pl.BlockSpec((None, T, LANES), lambda b, h, n: (b, 0, h))],
        out_specs=pl.BlockSpec((None, QB, step_cols), lambda b, h, n: (b, n + q_block0, h)),
        out_shape=jax.ShapeDtypeStruct((B, T, nq), bf16),
        compiler_params=_params(3),
        name="attn_window",
    )(sink, q, kt, v)


B_CHUNK_TILES = 2
B_STEP_HEADS = 4


def _attn_b_kernel(lam_ref, q_ref, kt_ref, v_ref, g_ref, o_ref, *, n_ctx, out_scale, q_block0):
    n = pl.program_id(2) + q_block0
    lam = lam_ref[0]
    n_tiles = kt_ref.shape[1]
    q = q_ref[...]
    units = [(h, r) for h in range(B_STEP_HEADS) for r in range(QB_B // QB)]
    chains = []
    for h, r in units:
        chains += _half_masked(q[r * QB:(r + 1) * QB, h * LANES:(h + 1) * LANES])

    def run(n_key_tiles):
        states = [None] * len(chains)
        for t0 in range(0, n_key_tiles, B_CHUNK_TILES):
            t1 = min(t0 + B_CHUNK_TILES, n_key_tiles)
            for u, (h, r) in enumerate(units):
                kt = jnp.concatenate([kt_ref[h, j] for j in range(t0, t1)], axis=1)
                v = v_ref[t0 * KB:t1 * KB, h * LANES:(h + 1) * LANES]
                for i in (2 * u, 2 * u + 1):
                    states[i] = _chain_update(states[i], chains[i], kt, v, None)
        for u, (h, r) in enumerate(units):
            (_, l1, a1), (_, l2, a2) = states[2 * u], states[2 * u + 1]
            od = a1 * (1.0 / l1) - lam * (a2 * (1.0 / l2))
            y = od * lax.rsqrt(jnp.mean(od * od, axis=1, keepdims=True) + RMS_EPS)
            o_ref[r * QB:(r + 1) * QB, h * LANES:(h + 1) * LANES] = (
                y * g_ref[...] * out_scale).astype(bf16)

    @pl.when(n < n_ctx // QB_B)
    def _():
        run(n_ctx // KB)

    @pl.when(n >= n_ctx // QB_B)
    def _():
        run(n_tiles)


def _attn_b(q, kt, v, lam, subln_g, n_ctx, out_scale, q_row0):
    B, T, nq = q.shape
    cols = B_STEP_HEADS * LANES
    q_block0 = q_row0 // QB_B
    return pl.pallas_call(
        functools.partial(_attn_b_kernel, n_ctx=n_ctx, out_scale=out_scale, q_block0=q_block0),
        grid=(B, nq // cols, T // QB_B - q_block0),
        in_specs=[pl.BlockSpec(memory_space=pltpu.SMEM),
                  pl.BlockSpec((None, QB_B, cols), lambda b, h, n: (b, n + q_block0, h)),
                  pl.BlockSpec((None, B_STEP_HEADS, T // KB, LANES, KB), lambda b, h, n: (b, h, 0, 0, 0)),
                  pl.BlockSpec((None, T, cols), lambda b, h, n: (b, 0, h)),
                  pl.BlockSpec((1, LANES), lambda b, h, n: (0, 0))],
        out_specs=pl.BlockSpec((None, QB_B, cols), lambda b, h, n: (b, n + q_block0, h)),
        out_shape=jax.ShapeDtypeStruct((B, T, nq), bf16),
        compiler_params=_params(3),
        name="attn_diff",
    )(lam, q, kt, v, subln_g)


def _residual_ln(x, y, gate, ln_g, ln_b):
    z = ALPHA * x + (1.0 + gate) * y
    mu = jnp.mean(z, axis=1, keepdims=True)
    zc = z - mu
    var = jnp.mean(zc * zc, axis=1, keepdims=True)
    return zc * lax.rsqrt(var + LN_EPS) * ln_g + ln_b


def _oproj_kernel(o_ref, x_ref, mod_ref, w_ref, g_ref, b_ref, xo_ref):
    for s in range(x_ref.shape[0] // SUB):
        rows = slice(s * SUB, (s + 1) * SUB)
        y = _dot(o_ref[rows, :], w_ref[...])
        xo_ref[rows, :] = _residual_ln(x_ref[rows, :], y, mod_ref[s, G_A:G_A + 1, :],
                                       g_ref[...], b_ref[...])


def _oproj(o, X, modtab, w_o, ln_g, ln_b, tm, row0):
    B, T, _ = X.shape
    blk0 = row0 // tm
    in_rows = pl.BlockSpec((None, tm, D), lambda b, t: (b, t + blk0, 0))
    vec_spec = pl.BlockSpec((1, D), lambda b, t: (0, 0))
    return pl.pallas_call(
        _oproj_kernel,
        grid=(B, (T - row0) // tm),
        in_specs=[in_rows, in_rows,
                  pl.BlockSpec((None, tm // SUB, N_MOD, D), lambda b, t: (b, t + blk0, 0, 0)),
                  pl.BlockSpec((D, D), lambda b, t: (0, 0)),
                  vec_spec, vec_spec],
        out_specs=pl.BlockSpec((None, tm, D), lambda b, t: (b, t, 0)),
        out_shape=jax.ShapeDtypeStruct((B, T - row0, D), f32),
        compiler_params=_params(2),
        name="oproj_ln",
    )(o, X, modtab, w_o, ln_g, ln_b)


FFN_FC = 1408


def _ffn_kernel(x_ref, mod_ref, wg_ref, wu_ref, wd_ref, lng_ref, lnb_ref, o_ref):
    n_hidden = wg_ref.shape[1]
    for s in range(x_ref.shape[0] // SUB):
        rows = slice(s * SUB, (s + 1) * SUB)
        x = x_ref[rows, :]
        h = (x * (1.0 + mod_ref[s, SC_F:SC_F + 1, :]) + mod_ref[s, SH_F:SH_F + 1, :]).astype(bf16)
        y = None
        for c0 in range(0, n_hidden, FFN_FC):
            cols = slice(c0, c0 + FFN_FC)
            g = _dot(h, wg_ref[:, cols])
            u = _dot(h, wu_ref[:, cols])
            a = g * (1.0 / (1.0 + jnp.exp(-g))) * u
            d = _dot(a.astype(bf16), wd_ref[cols, :])
            y = d if y is None else y + d
        o_ref[rows, :] = _residual_ln(x, y, mod_ref[s, G_F:G_F + 1, :], lng_ref[...], lnb_ref[...])


def _ffn(X, modtab, wg, wu, wd, ln_g, ln_b, tm):
    B, T, _ = X.shape
    n_hidden = wg.shape[1]
    row_spec = pl.BlockSpec((None, tm, D), lambda b, t: (b, t, 0))
    vec_spec = pl.BlockSpec((1, D), lambda b, t: (0, 0))
    once = pl.Buffered(1)
    return pl.pallas_call(
        _ffn_kernel,
        grid=(B, T // tm),
        in_specs=[row_spec, pl.BlockSpec((None, tm // SUB, N_MOD, D), lambda b, t: (b, t, 0, 0)),
                  pl.BlockSpec((D, n_hidden), lambda b, t: (0, 0), pipeline_mode=once),
                  pl.BlockSpec((D, n_hidden), lambda b, t: (0, 0), pipeline_mode=once),
                  pl.BlockSpec((n_hidden, D), lambda b, t: (0, 0), pipeline_mode=once),
                  vec_spec, vec_spec],
        out_specs=row_spec,
        out_shape=jax.ShapeDtypeStruct((B, T, D), f32),
        compiler_params=_params(2),
        name="ffn_dense",
    )(X, modtab, wg, wu, wd, ln_g, ln_b)


CNT_ROWS = 8


def _router_kernel(x_ref, mod_ref, wr_ref, tri_ref, h_ref, gates_ref, slots_ref, cnt_ref, carry_scr):
    t = pl.program_id(1)

    @pl.when(t == 0)
    def _():
        carry_scr[...] = jnp.zeros_like(carry_scr)

    lane = lax.broadcasted_iota(jnp.int32, (SUB, LANES), 1)
    nsub = x_ref.shape[0] // SUB
    starts = []
    for s in range(nsub):
        rows = slice(s * SUB, (s + 1) * SUB)
        h = _modulated(x_ref, mod_ref, s, SH_F, SC_F)
        h_ref[rows, :] = h.astype(bf16)
        logits = jnp.dot(h, wr_ref[...], preferred_element_type=f32,
                         precision=lax.Precision.HIGHEST)
        lg = jnp.where(lane < N_EXPERTS, logits, -jnp.inf)
        v1 = jnp.max(lg, axis=1, keepdims=True)
        i1 = jnp.min(jnp.where(lg == v1, lane, LANES), axis=1, keepdims=True)
        lg2 = jnp.where(lane == i1, -jnp.inf, lg)
        v2 = jnp.max(lg2, axis=1, keepdims=True)
        i2 = jnp.min(jnp.where(lg2 == v2, lane, LANES), axis=1, keepdims=True)
        ex = jnp.exp(v2 - v1)
        w1 = 1.0 / (1.0 + ex)
        w2 = ex / (1.0 + ex)
        sel = (lane == i1) | (lane == i2)
        gates_ref[rows, :] = jnp.where(lane == i1, w1, jnp.where(lane == i2, w2, 0.0))
        cum = _dot(tri_ref[...], jnp.where(sel, 1.0, 0.0).astype(bf16))
        carry = carry_scr[...]
        starts.append(carry)
        slots_ref[rows, :] = jnp.where(sel, carry + cum - 1.0, -1.0)
        carry_scr[...] = carry + cum[SUB - 1:SUB, :]
    pad = jnp.zeros((CNT_ROWS - nsub - 1, LANES), f32)
    cnt_ref[...] = jnp.concatenate(starts + [carry_scr[...], pad], axis=0)


def _router(X, modtab, w_router_padded, tri, tm):
    B, T, _ = X.shape
    nt = T // tm
    assert tm // SUB < CNT_ROWS
    row128 = pl.BlockSpec((None, tm, LANES), lambda b, t: (b, t, 0))
    return pl.pallas_call(
        _router_kernel,
        grid=(B, nt),
        in_specs=[pl.BlockSpec((None, tm, D), lambda b, t: (b, t, 0)),
                  pl.BlockSpec((None, tm // SUB, N_MOD, D), lambda b, t: (b, t, 0, 0)),
                  pl.BlockSpec((D, LANES), lambda b, t: (0, 0)),
                  pl.BlockSpec((SUB, SUB), lambda b, t: (0, 0))],
        out_specs=[pl.BlockSpec((None, tm, D), lambda b, t: (b, t, 0)), row128, row128,
                   pl.BlockSpec((None, None, CNT_ROWS, LANES), lambda b, t: (b, t, 0, 0))],
        out_shape=[jax.ShapeDtypeStruct((B, T, D), bf16),
                   jax.ShapeDtypeStruct((B, T, LANES), f32),
                   jax.ShapeDtypeStruct((B, T, LANES), f32),
                   jax.ShapeDtypeStruct((B, nt, CNT_ROWS, LANES), f32)],
        scratch_shapes=[pltpu.VMEM((1, LANES), f32)],
        compiler_params=_params(2),
        name="router",
    )(X, modtab, w_router_padded, tri)


CH = 128
MOE_FC = 1792
GATHER_GROUPS = 3
SCATTER_CHUNKS = 2


def _moe_kernel(nch_ref, klo_ref, khi_ref, clo_ref, chi_ref,
                h_ref, slots_ref, slots_t_ref, gates_t_ref, wg_ref, wu_ref, wd_ref,
                y_ref, xs_scr, ys_scr, gate_scr, *, max_chunks, n_groups, n_f):
    b = pl.program_id(0)
    e = pl.program_id(1)
    f = pl.program_id(2)
    be = b * pl.num_programs(1) + e
    nch = nch_ref[be]

    @pl.when((b == 0) & (e == 0) & (f == 0))
    def _():
        xs_scr[...] = jnp.zeros_like(xs_scr)
        ys_scr[...] = jnp.zeros_like(ys_scr)
        gate_scr[...] = jnp.zeros_like(gate_scr)

    @pl.when((e == 0) & (f == 0))
    def _():
        y_ref[...] = jnp.zeros_like(y_ref)

    @pl.when(f == 0)
    def _gather():
        def chunk(c, carry):
            rows = pl.ds(pl.multiple_of(c * CH, CH), CH)
            rank = (lax.broadcasted_iota(jnp.int32, (CH, SUB), 0) + c * CH).astype(f32)
            lo = klo_ref[be * max_chunks + c]
            hi = khi_ref[be * max_chunks + c]
            g0 = jnp.minimum(lo, n_groups - GATHER_GROUPS)
            hits = [slots_t_ref[e, pl.ds(g0 + j, 1), :] == rank for j in range(GATHER_GROUPS)]
            p = jnp.concatenate([jnp.where(hit, 1.0, 0.0).astype(bf16) for hit in hits], axis=1)
            window = h_ref[pl.ds(pl.multiple_of(g0 * SUB, SUB), GATHER_GROUPS * SUB), :]
            xs_scr[rows, :] = _dot(p, window).astype(bf16)
            gate = jnp.zeros((CH, 1), f32)
            for j, hit in enumerate(hits):
                gate = gate + jnp.sum(jnp.where(hit, gates_t_ref[e, pl.ds(g0 + j, 1), :], 0.0),
                                      axis=1, keepdims=True)
            gate_scr[rows, :] = gate

            @pl.when(hi >= g0 + GATHER_GROUPS)
            def _():
                def group(kb, acc):
                    xacc, gacc = acc
                    hit = slots_t_ref[e, pl.ds(kb, 1), :] == rank
                    xacc = xacc + _dot(jnp.where(hit, 1.0, 0.0).astype(bf16),
                                       h_ref[pl.ds(pl.multiple_of(kb * SUB, SUB), SUB), :])
                    gacc = gacc + jnp.sum(jnp.where(hit, gates_t_ref[e, pl.ds(kb, 1), :], 0.0),
                                          axis=1, keepdims=True)
                    return xacc, gacc

                xacc, gacc = lax.fori_loop(lo, hi + 1, group,
                                           (jnp.zeros((CH, D), f32), jnp.zeros((CH, 1), f32)))
                xs_scr[rows, :] = xacc.astype(bf16)
                gate_scr[rows, :] = gacc

            return carry

        lax.fori_loop(0, nch, chunk, 0)

    def expert_rows(r0, m, stage):
        rows = pl.ds(r0, m)
        xs = xs_scr[rows, :]
        g = _dot(xs, wg_ref[...])
        u = _dot(xs, wu_ref[...])
        a = g * (1.0 / (1.0 + jnp.exp(-g))) * u * gate_scr[rows, :]
        d = _dot(a.astype(bf16), wd_ref[...])
        if stage == "first":
            ys_scr[rows, :] = d
        elif stage == "middle":
            ys_scr[rows, :] += d
        else:
            xs_scr[rows, :] = (ys_scr[rows, :] + d).astype(bf16)

    def run_experts(stage):
        def pair(i, carry):
            expert_rows(pl.multiple_of(i * 2 * CH, 2 * CH), 2 * CH, stage)
            return carry

        lax.fori_loop(0, nch // 2, pair, 0)

        @pl.when(nch % 2 == 1)
        def _():
            expert_rows(pl.multiple_of((nch - 1) * CH, CH), CH, stage)

    @pl.when(f == 0)
    def _():
        run_experts("first")

    if n_f > 2:
        @pl.when((f > 0) & (f < n_f - 1))
        def _():
            run_experts("middle")

    @pl.when(f == n_f - 1)
    def _scatter():
        run_experts("last")
        lane_e = lax.broadcasted_iota(jnp.int32, (SUB, LANES), 1)
        lane_w = lax.broadcasted_iota(jnp.int32, (SUB, SCATTER_CHUNKS * CH), 1)
        lane_c = lax.broadcasted_iota(jnp.int32, (SUB, CH), 1)

        def expert_col(rows):
            return jnp.sum(jnp.where(lane_e == e, slots_ref[rows, :], 0.0), axis=1, keepdims=True)

        for kb in range(n_groups):
            rows = slice(kb * SUB, (kb + 1) * SUB)
            c0 = jnp.minimum(clo_ref[be * n_groups + kb], max_chunks - SCATTER_CHUNKS)
            pt = jnp.where(expert_col(rows) == (lane_w + c0 * CH).astype(f32), 1.0, 0.0).astype(bf16)
            window = xs_scr[pl.ds(pl.multiple_of(c0 * CH, CH), SCATTER_CHUNKS * CH), :]
            y_ref[rows, :] += _dot(pt, window)

        def spill(kb, carry):
            rows = pl.ds(pl.multiple_of(kb * SUB, SUB), SUB)
            c0 = jnp.minimum(clo_ref[be * n_groups + kb], max_chunks - SCATTER_CHUNKS)

            def chunk(c, carry2):
                pt = jnp.where(expert_col(rows) == (lane_c + c * CH).astype(f32), 1.0, 0.0).astype(bf16)
                y_ref[rows, :] += _dot(pt, xs_scr[pl.ds(pl.multiple_of(c * CH, CH), CH), :])
                return carry2

            lax.fori_loop(c0 + SCATTER_CHUNKS, chi_ref[be * n_groups + kb] + 1, chunk, 0)
            return carry

        lax.fori_loop(0, n_groups, spill, 0)


def _moe(h, slots, slots_t, gates_t, tables, wg, wu, wd):
    B, T, _ = h.shape
    n_e, _, n_hidden = wg.shape
    max_chunks = T // CH
    n_groups = T // SUB
    n_f = n_hidden // MOE_FC
    assert n_f >= 2 and n_f * MOE_FC == n_hidden
    once = pl.Buffered(1)
    grid_spec = pltpu.PrefetchScalarGridSpec(
        num_scalar_prefetch=len(tables),
        grid=(B, n_e, n_f),
        in_specs=[pl.BlockSpec((None, T, D), lambda b, e, f, *_: (b, 0, 0), pipeline_mode=once),
                  pl.BlockSpec((None, T, LANES), lambda b, e, f, *_: (b, 0, 0), pipeline_mode=once),
                  pl.BlockSpec((None, n_e, n_groups, SUB), lambda b, e, f, *_: (b, 0, 0, 0)),
                  pl.BlockSpec((None, n_e, n_groups, SUB), lambda b, e, f, *_: (b, 0, 0, 0)),
                  pl.BlockSpec((None, D, MOE_FC), lambda b, e, f, *_: (e, 0, f)),
                  pl.BlockSpec((None, D, MOE_FC), lambda b, e, f, *_: (e, 0, f)),
                  pl.BlockSpec((None, MOE_FC, D), lambda b, e, f, *_: (e, f, 0))],
        out_specs=pl.BlockSpec((None, T, D), lambda b, e, f, *_: (b, 0, 0), pipeline_mode=once),
        scratch_shapes=[pltpu.VMEM((T, D), bf16), pltpu.VMEM((T, D), f32),
                        pltpu.VMEM((T, 1), f32)])
    return pl.pallas_call(
        functools.partial(_moe_kernel, max_chunks=max_chunks, n_groups=n_groups, n_f=n_f),
        grid_spec=grid_spec,
        out_shape=jax.ShapeDtypeStruct((B, T, D), f32),
        compiler_params=_params(3),
        name="moe_routed",
    )(*tables, h, slots, slots_t, gates_t, wg, wu, wd)


def _routing_tables(slots, gates, cnt, tm):
    B, T, _ = slots.shape
    nsub = tm // SUB
    n_groups = T // SUB
    max_chunks = T // CH

    def expert_major(a):
        return a[:, :, :N_EXPERTS].reshape(B, n_groups, SUB, N_EXPERTS).transpose(0, 3, 1, 2)

    starts = cnt[:, :, :nsub, :N_EXPERTS].reshape(B, n_groups, N_EXPERTS).astype(jnp.int32)
    starts = starts.transpose(0, 2, 1)
    total = cnt[:, -1, nsub, :N_EXPERTS].astype(jnp.int32)
    nch = (total + CH - 1) // CH
    r0 = jnp.arange(max_chunks, dtype=jnp.int32)[None, None, :] * CH
    r1 = jnp.minimum(total[:, :, None], r0 + CH) - 1
    st = starts[:, :, None, :]
    klo = jnp.sum(st <= r0[..., None], axis=-1) - 1
    khi = jnp.sum(st <= r1[..., None], axis=-1) - 1
    live = r0 < total[:, :, None]
    klo = jnp.where(live, klo, 0)
    khi = jnp.where(live, khi, -1)
    ends = jnp.concatenate([starts[:, :, 1:], total[:, :, None]], axis=-1)
    clo = starts // CH
    chi = jnp.where(ends > starts, (ends - 1) // CH, clo - 1)
    tables = tuple(a.astype(jnp.int32).reshape(-1) for a in (nch, klo, khi, clo, chi))
    return expert_major(slots), expert_major(gates), tables


def _resid_ln_kernel(x_ref, y_ref, mod_ref, g_ref, b_ref, o_ref):
    for s in range(x_ref.shape[0] // SUB):
        rows = slice(s * SUB, (s + 1) * SUB)
        o_ref[rows, :] = _residual_ln(x_ref[rows, :], y_ref[rows, :], mod_ref[s, G_F:G_F + 1, :],
                                      g_ref[...], b_ref[...])


def _resid_ln(X, y, modtab, ln_g, ln_b, tm):
    B, T, _ = X.shape
    row_spec = pl.BlockSpec((None, tm, D), lambda b, t: (b, t, 0))
    vec_spec = pl.BlockSpec((1, D), lambda b, t: (0, 0))
    return pl.pallas_call(
        _resid_ln_kernel,
        grid=(B, T // tm),
        in_specs=[row_spec, row_spec,
                  pl.BlockSpec((None, tm // SUB, N_MOD, D), lambda b, t: (b, t, 0, 0)),
                  vec_spec, vec_spec],
        out_specs=row_spec,
        out_shape=jax.ShapeDtypeStruct((B, T, D), f32),
        compiler_params=_params(2),
        name="resid_ln",
    )(X, y, modtab, ln_g, ln_b)


def _rope_tables(n_ctx, n_lat):
    rows = n_lat // GRID_W
    row = jnp.repeat(jnp.arange(rows, dtype=f32), GRID_W)
    col = jnp.tile(jnp.arange(GRID_W, dtype=f32), rows)
    half = HEAD_DIM // 2
    inv = ROPE_THETA ** (-jnp.arange(0, half, 2, dtype=f32) / half)
    ang_r = row[:, None] * inv[None, :]
    ang_c = col[:, None] * inv[None, :]
    ang = jnp.concatenate([ang_r, ang_r, ang_c, ang_c], axis=-1)
    cos, sin = jnp.cos(ang), jnp.sin(ang)
    quarter = (np.arange(HEAD_DIM) // (HEAD_DIM // 4)) % 2
    sa = jnp.where(quarter == 0, -sin, 0.0)
    sb = jnp.where(quarter == 1, sin, 0.0)
    reps = LANES // HEAD_DIM

    def full(tab, ctx_value):
        tab = jnp.concatenate([jnp.full((n_ctx, HEAD_DIM), ctx_value, f32), tab], axis=0)
        return jnp.tile(tab, (1, reps))

    return full(cos, 1.0), full(sa, 0.0), full(sb, 0.0)


def _a_head_perm():
    group = A_HEADS // A_KV_HEADS
    order = []
    for hp in range(A_KV_HEADS // 2):
        for j in range(group):
            order += [(2 * hp) * group + j, (2 * hp + 1) * group + j]
    return np.asarray(order)


def kernel(x, c, ctx, c_ctx, w_mod, b_mod, ln_g, ln_b, a_w_qkv, a_sink, a_w_o,
           b_w_qkv, b_lam_q1, b_lam_k1, b_lam_q2, b_lam_k2, b_subln_g, b_w_o,
           ff_w_gate, ff_w_up, ff_w_down, moe_w_router, moe_w_gate, moe_w_up, moe_w_down):
    B, L, _ = x.shape
    C = ctx.shape[1]
    X = jnp.concatenate([ctx, x], axis=1)

    cond = jnp.concatenate([c, c_ctx[None, :], jnp.zeros((MOD_ROWS - B - 1, D), f32)], axis=0)
    mods = _modulation(cond, w_mod, b_mod).reshape(DEPTH, MOD_ROWS, N_MOD, D)
    rope_tabs = _rope_tables(C, L)
    tri = jnp.asarray(np.tril(np.ones((SUB, SUB), np.float32)), bf16)
    perm = _a_head_perm()
    perm_cols = (perm[:, None] * HEAD_DIM + np.arange(HEAD_DIM)[None, :]).reshape(-1)
    qscale = HEAD_DIM ** -0.5 * LOG2E
    a_qd = A_HEADS * HEAD_DIM
    a_kd = A_KV_HEADS * HEAD_DIM
    b_qd = 2 * B_HEADS * HEAD_DIM

    for i in range(DEPTH):
        j = i // 2
        with_ctx = i < DEPTH - 1
        m_lat = jnp.broadcast_to(mods[i, :B, None], (B, L // SUB, N_MOD, D))
        m_ctx = jnp.broadcast_to(mods[i, B][None, None], (B, C // SUB, N_MOD, D))
        modtab = jnp.concatenate([m_ctx, m_lat], axis=1)
        q_row0 = 0 if with_ctx else C

        if i % 2 == 0:
            w = a_w_qkv[j]
            wq = w[:, :a_qd][:, perm_cols].astype(bf16)
            wk = w[:, a_qd:a_qd + a_kd].astype(bf16)
            wv = w[:, a_qd + a_kd:].astype(bf16)
            q, k, v = _qkv(X, modtab, rope_tabs, wq, wk, wv, qscale)
            sink = a_sink[j][perm].astype(f32) * LOG2E
            o = _attn_a(q, k, v, sink, C, q_row0)
            w_o = a_w_o[j][perm_cols, :].astype(bf16)
        else:
            w = b_w_qkv[j]
            wq = w[:, :b_qd].astype(bf16)
            wk = w[:, b_qd:2 * b_qd].astype(bf16)
            wv = w[:, 2 * b_qd:].astype(bf16)
            q, k, v = _qkv(X, modtab, rope_tabs, wq, wk, wv, qscale)
            lam_init = 0.8 - 0.6 * math.exp(-0.3 * i)
            lam = (jnp.exp(jnp.sum(b_lam_q1[j] * b_lam_k1[j]))
                   - jnp.exp(jnp.sum(b_lam_q2[j] * b_lam_k2[j])) + lam_init).reshape(1)
            o = _attn_b(q, k, v, lam, b_subln_g[j].reshape(1, B_V_DIM), C, 1.0 - lam_init, q_row0)
            w_o = b_w_o[j].astype(bf16)
        lng0 = ln_g[i, 0].reshape(1, D)
        lnb0 = ln_b[i, 0].reshape(1, D)
        if with_ctx:
            X = _oproj(o, X, modtab, w_o, lng0, lnb0, TM, 0)
            tm = TM
        else:
            X = _oproj(o, X, modtab, w_o, lng0, lnb0, SUB, C)
            modtab = m_lat
            tm = TM_LATENT

        lng = ln_g[i, 1].reshape(1, D)
        lnb = ln_b[i, 1].reshape(1, D)
        if i % 2 == 0:
            X = _ffn(X, modtab, ff_w_gate[j].astype(bf16), ff_w_up[j].astype(bf16),
                     ff_w_down[j].astype(bf16), lng, lnb, tm)
        else:
            wr = jnp.pad(moe_w_router[j], ((0, 0), (0, LANES - N_EXPERTS)))
            h, gates, slots, cnt = _router(X, modtab, wr, tri, tm)
            slots_t, gates_t, tables = _routing_tables(slots, gates, cnt, tm)
            y = _moe(h, slots, slots_t, gates_t, tables, moe_w_gate[j].astype(bf16),
                     moe_w_up[j].astype(bf16), moe_w_down[j].astype(bf16))
            X = _resid_ln(X, y, modtab, lng, lnb, tm)
    return X
```

```python
import functools
import math

import numpy as np
import jax
import jax.numpy as jnp
from jax import lax
from jax.experimental import pallas as pl
from jax.experimental.pallas import tpu as pltpu

f32 = jnp.float32
bf16 = jnp.bfloat16

D = 1024
DEPTH = 4
GRID_W = 64
N_MOD = 6
HEAD_DIM = 64
A_HEADS = 16
A_KV_HEADS = 4
WINDOW = 128
B_HEADS = 8
B_V_DIM = 128
N_EXPERTS = 8
ROPE_THETA = 10000.0
LN_EPS = 1e-5
RMS_EPS = 1e-5
NEG_INF = -1e30
ALPHA = (2.0 * DEPTH) ** 0.25
LOG2E = 1.4426950408889634

LANES = 128
SUB = 256
TM = 768
NSUB = TM // SUB
TM_LATENT = 1024
QB = 128
KB = 128
QB_B = 256
VMEM_LIMIT = 56 * 1024 * 1024

SH_A, SC_A, G_A, SH_F, SC_F, G_F = range(6)


def _dot(a, b):
    return jnp.dot(a, b, preferred_element_type=f32)


def _params(n_grid):
    return pltpu.CompilerParams(dimension_semantics=("arbitrary",) * n_grid,
                                vmem_limit_bytes=VMEM_LIMIT)


MOD_ROWS = 24
MOD_NB = 1536


def _mod_kernel(c_ref, w_ref, b_ref, o_ref):
    cond = c_ref[...]
    s = cond * (1.0 / (1.0 + jnp.exp(-cond)))
    o_ref[...] = _dot(s.astype(bf16), w_ref[...].astype(bf16)) + b_ref[...]


def _modulation(cond, w_mod, b_mod):
    n = N_MOD * D
    return pl.pallas_call(
        _mod_kernel,
        grid=(DEPTH, n // MOD_NB),
        in_specs=[pl.BlockSpec((MOD_ROWS, D), lambda i, j: (0, 0)),
                  pl.BlockSpec((None, D, MOD_NB), lambda i, j: (i, 0, j)),
                  pl.BlockSpec((None, 1, MOD_NB), lambda i, j: (i, 0, j))],
        out_specs=pl.BlockSpec((None, MOD_ROWS, MOD_NB), lambda i, j: (i, 0, j)),
        out_shape=jax.ShapeDtypeStruct((DEPTH, MOD_ROWS, n), f32),
        compiler_params=_params(2),
        name="modulation",
    )(cond, w_mod, b_mod.reshape(DEPTH, 1, n))


def _modulated(x_ref, mod_ref, s, sh, sc):
    x = x_ref[s * SUB:(s + 1) * SUB, :]
    return x * (1.0 + mod_ref[s, sc:sc + 1, :]) + mod_ref[s, sh:sh + 1, :]


def _qkv_kernel(x_ref, mod_ref, cos_ref, sa_ref, sb_ref, wq_ref, wk_ref, wv_ref,
                q_ref, kt_ref, v_ref, *, qscale):
    for s in range(x_ref.shape[0] // SUB):
        rows = slice(s * SUB, (s + 1) * SUB)
        h = _modulated(x_ref, mod_ref, s, SH_A, SC_A).astype(bf16)
        cos = cos_ref[rows, :]
        sa = sa_ref[rows, :]
        sb = sb_ref[rows, :]

        def rope(y):
            return (y * cos + pltpu.roll(y, LANES - 16, 1) * sa + pltpu.roll(y, 16, 1) * sb)

        yq = _dot(h, wq_ref[...])
        for j in range(yq.shape[1] // LANES):
            blk = slice(j * LANES, (j + 1) * LANES)
            q_ref[rows, blk] = (rope(yq[:, blk]) * qscale).astype(bf16)
        yk = _dot(h, wk_ref[...])
        for j in range(yk.shape[1] // LANES):
            kr = rope(yk[:, j * LANES:(j + 1) * LANES])
            for r in range(SUB // KB):
                kt_ref[j, s * (SUB // KB) + r] = kr[r * KB:(r + 1) * KB, :].T.astype(bf16)
        v_ref[rows, :] = _dot(h, wv_ref[...]).astype(bf16)


def _qkv(X, modtab, rope_tabs, wq, wk, wv, qscale):
    B, T, _ = X.shape
    nq, nk, nv = wq.shape[1], wk.shape[1], wv.shape[1]
    cos, sa, sb = rope_tabs
    tab_spec = pl.BlockSpec((TM, LANES), lambda b, t: (t, 0))
    return pl.pallas_call(
        functools.partial(_qkv_kernel, qscale=qscale),
        grid=(B, T // TM),
        in_specs=[pl.BlockSpec((None, TM, D), lambda b, t: (b, t, 0)),
                  pl.BlockSpec((None, NSUB, N_MOD, D), lambda b, t: (b, t, 0, 0)),
                  tab_spec, tab_spec, tab_spec,
                  pl.BlockSpec((D, nq), lambda b, t: (0, 0)),
                  pl.BlockSpec((D, nk), lambda b, t: (0, 0)),
                  pl.BlockSpec((D, nv), lambda b, t: (0, 0))],
        out_specs=[pl.BlockSpec((None, TM, nq), lambda b, t: (b, t, 0)),
                   pl.BlockSpec((None, nk // LANES, TM // KB, LANES, KB), lambda b, t: (b, 0, t, 0, 0)),
                   pl.BlockSpec((None, TM, nv), lambda b, t: (b, t, 0))],
        out_shape=[jax.ShapeDtypeStruct((B, T, nq), bf16),
                   jax.ShapeDtypeStruct((B, nk // LANES, T // KB, LANES, KB), bf16),
                   jax.ShapeDtypeStruct((B, T, nv), bf16)],
        compiler_params=_params(2),
        name="qkv_proj",
    )(X, modtab, cos, sa, sb, wq, wk, wv)


A_PAIRS = 4


def _chain_update(state, q, kt, v, mask):
    s = _dot(q, kt)
    if mask is not None:
        s = jnp.where(mask, s, NEG_INF)
    if state is None:
        m_new = jnp.max(s, axis=1, keepdims=True)
        p = jnp.exp2(s - m_new)
        return m_new, jnp.sum(p, axis=1, keepdims=True), _dot(p.astype(bf16), v)
    m, l, acc = state
    m_new = jnp.maximum(m, jnp.max(s, axis=1, keepdims=True))
    alpha = jnp.exp2(m - m_new)
    p = jnp.exp2(s - m_new)
    return (m_new, alpha * l + jnp.sum(p, axis=1, keepdims=True),
            alpha * acc + _dot(p.astype(bf16), v))


def _half_masked(q):
    lo = lax.broadcasted_iota(jnp.int32, q.shape, 1) < HEAD_DIM
    zero = jnp.zeros_like(q)
    return [jnp.where(lo, q, zero), jnp.where(lo, zero, q)]


def _attn_a_kernel(sink_ref, q_ref, kt_ref, v_ref, o_ref, *, n_ctx, n_lat, q_block0):
    n = pl.program_id(1) + q_block0
    n_ctx_tiles = n_ctx // KB
    n_kv_pairs = kt_ref.shape[0]
    lo = lax.broadcasted_iota(jnp.int32, (QB, LANES), 1) < HEAD_DIM
    q = q_ref[...]
    chains = []
    for p in range(n_kv_pairs * A_PAIRS):
        chains += _half_masked(q[:, p * LANES:(p + 1) * LANES])

    def init(i):
        return (jnp.full((QB, 1), sink_ref[i], f32),
                jnp.ones((QB, 1), f32), jnp.zeros((QB, LANES), f32))

    def finish(states):
        for p in range(n_kv_pairs * A_PAIRS):
            (_, l0, a0), (_, l1, a1) = states[2 * p], states[2 * p + 1]
            o_ref[:, p * LANES:(p + 1) * LANES] = jnp.where(lo, a0 * (1.0 / l0),
                                                          a1 * (1.0 / l1)).astype(bf16)

    def keys(hp, tiles):
        return jnp.concatenate([kt_ref[hp, j] for j in tiles], axis=1)

    def vals(hp, row0, n_rows):
        return v_ref[pl.ds(row0, n_rows), hp * LANES:(hp + 1) * LANES]

    @pl.when(n < n_ctx // QB)
    def _():
        states = []
        for i in range(len(chains)):
            hp = i // (2 * A_PAIRS)
            states.append(_chain_update(init(i), chains[i], keys(hp, range(n_ctx_tiles)),
                                        vals(hp, 0, n_ctx), None))
        finish(states)

    @pl.when(n >= n_ctx // QB)
    def _():
        nl = n - n_ctx // QB
        first = jnp.clip(nl - 1, 0, n_lat // KB - 3)
        jb = n_ctx_tiles + first
        row0 = pl.multiple_of(n_ctx + first * KB, KB)
        rel = (first * KB + lax.broadcasted_iota(jnp.int32, (QB, 3 * KB), 1)
               - nl * QB - lax.broadcasted_iota(jnp.int32, (QB, 3 * KB), 0))
        ok = jnp.abs(rel) <= WINDOW
        states = []
        for i in range(len(chains)):
            hp = i // (2 * A_PAIRS)
            st = _chain_update(init(i), chains[i], keys(hp, range(n_ctx_tiles)),
                               vals(hp, 0, n_ctx), None)
            st = _chain_update(st, chains[i], keys(hp, (jb, jb + 1)),
                               vals(hp, row0, 2 * KB), ok[:, :2 * KB])
            st = _chain_update(st, chains[i], keys(hp, (jb + 2,)),
                               vals(hp, row0 + 2 * KB, KB), ok[:, 2 * KB:])
            states.append(st)
        finish(states)


def _attn_a(q, kt, v, sink, n_ctx, q_row0):
    B, T, nq = q.shape
    n_lat = T - n_ctx
    n_kv_pairs = kt.shape[1]
    q_block0 = q_row0 // QB
    return pl.pallas_call(
        functools.partial(_attn_a_kernel, n_ctx=n_ctx, n_lat=n_lat, q_block0=q_block0),
        grid=(B, T // QB - q_block0),
        in_specs=[pl.BlockSpec(memory_space=pltpu.SMEM),
                  pl.BlockSpec((None, QB, nq), lambda b, n: (b, n + q_block0, 0)),
                  pl.BlockSpec((None, n_kv_pairs, T // KB, LANES, KB), lambda b, n: (b, 0, 0, 0, 0)),
                  pl.BlockSpec((None, T, n_kv_pairs * LANES), lambda b, n: (b, 0, 0))],
        out_specs=pl.BlockSpec((None, QB, nq), lambda b, n: (b, n + q_block0, 0)),
        out_shape=jax.ShapeDtypeStruct((B, T, nq), bf16),
        compiler_params=_params(2),
        name="attn_window",
    )(sink, q, kt, v)


B_CHUNK_TILES = 2
B_STEP_HEADS = 4


def _attn_b_kernel(lam_ref, q_ref, kt_ref, v_ref, g_ref, o_ref, *, n_ctx, out_scale, q_block0):
    n = pl.program_id(2) + q_block0
    lam = lam_ref[0]
    n_tiles = kt_ref.shape[1]
    q = q_ref[...]
    units = [(h, r) for h in range(B_STEP_HEADS) for r in range(QB_B // QB)]
    chains = []
    for h, r in units:
        chains += _half_masked(q[r * QB:(r + 1) * QB, h * LANES:(h + 1) * LANES])

    def run(n_key_tiles):
        states = [None] * len(chains)
        for t0 in range(0, n_key_tiles, B_CHUNK_TILES):
            t1 = min(t0 + B_CHUNK_TILES, n_key_tiles)
            for u, (h, r) in enumerate(units):
                kt = jnp.concatenate([kt_ref[h, j] for j in range(t0, t1)], axis=1)
                v = v_ref[t0 * KB:t1 * KB, h * LANES:(h + 1) * LANES]
                for i in (2 * u, 2 * u + 1):
                    states[i] = _chain_update(states[i], chains[i], kt, v, None)
        for u, (h, r) in enumerate(units):
            (_, l1, a1), (_, l2, a2) = states[2 * u], states[2 * u + 1]
            od = a1 * (1.0 / l1) - lam * (a2 * (1.0 / l2))
            y = od * lax.rsqrt(jnp.mean(od * od, axis=1, keepdims=True) + RMS_EPS)
            o_ref[r * QB:(r + 1) * QB, h * LANES:(h + 1) * LANES] = (
                y * g_ref[...] * out_scale).astype(bf16)

    @pl.when(n < n_ctx // QB_B)
    def _():
        run(n_ctx // KB)

    @pl.when(n >= n_ctx // QB_B)
    def _():
        run(n_tiles)


def _attn_b(q, kt, v, lam, subln_g, n_ctx, out_scale, q_row0):
    B, T, nq = q.shape
    cols = B_STEP_HEADS * LANES
    q_block0 = q_row0 // QB_B
    return pl.pallas_call(
        functools.partial(_attn_b_kernel, n_ctx=n_ctx, out_scale=out_scale, q_block0=q_block0),
        grid=(B, nq // cols, T // QB_B - q_block0),
        in_specs=[pl.BlockSpec(memory_space=pltpu.SMEM),
                  pl.BlockSpec((None, QB_B, cols), lambda b, h, n: (b, n + q_block0, h)),
                  pl.BlockSpec((None, B_STEP_HEADS, T // KB, LANES, KB), lambda b, h, n: (b, h, 0, 0, 0)),
                  pl.BlockSpec((None, T, cols), lambda b, h, n: (b, 0, h)),
                  pl.BlockSpec((1, LANES), lambda b, h, n: (0, 0))],
        out_specs=pl.BlockSpec((None, QB_B, cols), lambda b, h, n: (b, n + q_block0, h)),
        out_shape=jax.ShapeDtypeStruct((B, T, nq), bf16),
        compiler_params=_params(3),
        name="attn_diff",
    )(lam, q, kt, v, subln_g)


def _residual_ln(x, y, gate, ln_g, ln_b):
    z = ALPHA * x + (1.0 + gate) * y
    mu = jnp.mean(z, axis=1, keepdims=True)
    zc = z - mu
    var = jnp.mean(zc * zc, axis=1, keepdims=True)
    return zc * lax.rsqrt(var + LN_EPS) * ln_g + ln_b


def _oproj_kernel(o_ref, x_ref, mod_ref, w_ref, g_ref, b_ref, xo_ref):
    for s in range(x_ref.shape[0] // SUB):
        rows = slice(s * SUB, (s + 1) * SUB)
        y = _dot(o_ref[rows, :], w_ref[...])
        xo_ref[rows, :] = _residual_ln(x_ref[rows, :], y, mod_ref[s, G_A:G_A + 1, :],
                                       g_ref[...], b_ref[...])


def _oproj(o, X, modtab, w_o, ln_g, ln_b, tm, row0):
    B, T, _ = X.shape
    blk0 = row0 // tm
    in_rows = pl.BlockSpec((None, tm, D), lambda b, t: (b, t + blk0, 0))
    vec_spec = pl.BlockSpec((1, D), lambda b, t: (0, 0))
    return pl.pallas_call(
        _oproj_kernel,
        grid=(B, (T - row0) // tm),
        in_specs=[in_rows, in_rows,
                  pl.BlockSpec((None, tm // SUB, N_MOD, D), lambda b, t: (b, t + blk0, 0, 0)),
                  pl.BlockSpec((D, D), lambda b, t: (0, 0)),
                  vec_spec, vec_spec],
        out_specs=pl.BlockSpec((None, tm, D), lambda b, t: (b, t, 0)),
        out_shape=jax.ShapeDtypeStruct((B, T - row0, D), f32),
        compiler_params=_params(2),
        name="oproj_ln",
    )(o, X, modtab, w_o, ln_g, ln_b)


FFN_FC = 1408


def _ffn_kernel(x_ref, mod_ref, wg_ref, wu_ref, wd_ref, lng_ref, lnb_ref, o_ref):
    n_hidden = wg_ref.shape[1]
    for s in range(x_ref.shape[0] // SUB):
        rows = slice(s * SUB, (s + 1) * SUB)
        x = x_ref[rows, :]
        h = (x * (1.0 + mod_ref[s, SC_F:SC_F + 1, :]) + mod_ref[s, SH_F:SH_F + 1, :]).astype(bf16)
        y = None
        for c0 in range(0, n_hidden, FFN_FC):
            cols = slice(c0, c0 + FFN_FC)
            g = _dot(h, wg_ref[:, cols])
            u = _dot(h, wu_ref[:, cols])
            a = g * (1.0 / (1.0 + jnp.exp(-g))) * u
            d = _dot(a.astype(bf16), wd_ref[cols, :])
            y = d if y is None else y + d
        o_ref[rows, :] = _residual_ln(x, y, mod_ref[s, G_F:G_F + 1, :], lng_ref[...], lnb_ref[...])


def _ffn(X, modtab, wg, wu, wd, ln_g, ln_b, tm):
    B, T, _ = X.shape
    n_hidden = wg.shape[1]
    row_spec = pl.BlockSpec((None, tm, D), lambda b, t: (b, t, 0))
    vec_spec = pl.BlockSpec((1, D), lambda b, t: (0, 0))
    once = pl.Buffered(1)
    return pl.pallas_call(
        _ffn_kernel,
        grid=(B, T // tm),
        in_specs=[row_spec, pl.BlockSpec((None, tm // SUB, N_MOD, D), lambda b, t: (b, t, 0, 0)),
                  pl.BlockSpec((D, n_hidden), lambda b, t: (0, 0), pipeline_mode=once),
                  pl.BlockSpec((D, n_hidden), lambda b, t: (0, 0), pipeline_mode=once),
                  pl.BlockSpec((n_hidden, D), lambda b, t: (0, 0), pipeline_mode=once),
                  vec_spec, vec_spec],
        out_specs=row_spec,
        out_shape=jax.ShapeDtypeStruct((B, T, D), f32),
        compiler_params=_params(2),
        name="ffn_dense",
    )(X, modtab, wg, wu, wd, ln_g, ln_b)


CNT_ROWS = 8


def _router_kernel(x_ref, mod_ref, wr_hi_ref, wr_lo_ref, tri_ref,
                   h_ref, gates_ref, slots_ref, cnt_ref, carry_scr):
    t = pl.program_id(1)

    @pl.when(t == 0)
    def _():
        carry_scr[...] = jnp.zeros_like(carry_scr)

    lane = lax.broadcasted_iota(jnp.int32, (SUB, LANES), 1)
    nsub = x_ref.shape[0] // SUB
    starts = []
    for s in range(nsub):
        rows = slice(s * SUB, (s + 1) * SUB)
        h = _modulated(x_ref, mod_ref, s, SH_F, SC_F)
        h_hi = h.astype(bf16)
        h_ref[rows, :] = h_hi
        h_lo = (h - h_hi.astype(f32)).astype(bf16)
        logits = (_dot(h_hi, wr_hi_ref[...]) + _dot(h_lo, wr_hi_ref[...])
                  + _dot(h_hi, wr_lo_ref[...]))
        lg = jnp.where(lane < N_EXPERTS, logits, -jnp.inf)
        v1 = jnp.max(lg, axis=1, keepdims=True)
        i1 = jnp.min(jnp.where(lg == v1, lane, LANES), axis=1, keepdims=True)
        lg2 = jnp.where(lane == i1, -jnp.inf, lg)
        v2 = jnp.max(lg2, axis=1, keepdims=True)
        i2 = jnp.min(jnp.where(lg2 == v2, lane, LANES), axis=1, keepdims=True)
        ex = jnp.exp(v2 - v1)
        w1 = 1.0 / (1.0 + ex)
        w2 = ex / (1.0 + ex)
        sel = (lane == i1) | (lane == i2)
        gates_ref[rows, :] = jnp.where(lane == i1, w1, jnp.where(lane == i2, w2, 0.0))
        cum = _dot(tri_ref[...], jnp.where(sel, 1.0, 0.0).astype(bf16))
        carry = carry_scr[...]
        starts.append(carry)
        slots_ref[rows, :] = jnp.where(sel, carry + cum - 1.0, -1.0)
        carry_scr[...] = carry + cum[SUB - 1:SUB, :]
    pad = jnp.zeros((CNT_ROWS - nsub - 1, LANES), f32)
    cnt_ref[...] = jnp.concatenate(starts + [carry_scr[...], pad], axis=0)


def _router(X, modtab, w_router, tri, tm):
    B, T, _ = X.shape
    nt = T // tm
    assert tm // SUB < CNT_ROWS
    w = jnp.pad(w_router, ((0, 0), (0, LANES - N_EXPERTS)))
    w_hi = w.astype(bf16)
    w_lo = (w - w_hi.astype(f32)).astype(bf16)
    row128 = pl.BlockSpec((None, tm, LANES), lambda b, t: (b, t, 0))
    w_spec = pl.BlockSpec((D, LANES), lambda b, t: (0, 0))
    return pl.pallas_call(
        _router_kernel,
        grid=(B, nt),
        in_specs=[pl.BlockSpec((None, tm, D), lambda b, t: (b, t, 0)),
                  pl.BlockSpec((None, tm // SUB, N_MOD, D), lambda b, t: (b, t, 0, 0)),
                  w_spec, w_spec,
                  pl.BlockSpec((SUB, SUB), lambda b, t: (0, 0))],
        out_specs=[pl.BlockSpec((None, tm, D), lambda b, t: (b, t, 0)), row128, row128,
                   pl.BlockSpec((None, None, CNT_ROWS, LANES), lambda b, t: (b, t, 0, 0))],
        out_shape=[jax.ShapeDtypeStruct((B, T, D), bf16),
                   jax.ShapeDtypeStruct((B, T, LANES), f32),
                   jax.ShapeDtypeStruct((B, T, LANES), f32),
                   jax.ShapeDtypeStruct((B, nt, CNT_ROWS, LANES), f32)],
        scratch_shapes=[pltpu.VMEM((1, LANES), f32)],
        compiler_params=_params(2),
        name="router",
    )(X, modtab, w_hi, w_lo, tri)


CAST_ROWS = 256


def _cast_kernel(w_ref, o_ref):
    o_ref[...] = w_ref[...].astype(bf16)


def _expert_weights_bf16(w, layer):
    _, n_e, n_r, n_c = w.shape
    return pl.pallas_call(
        _cast_kernel,
        grid=(n_e, n_r // CAST_ROWS),
        in_specs=[pl.BlockSpec((None, None, CAST_ROWS, n_c), lambda e, r: (layer, e, r, 0))],
        out_specs=pl.BlockSpec((None, CAST_ROWS, n_c), lambda e, r: (e, r, 0)),
        out_shape=jax.ShapeDtypeStruct((n_e, n_r, n_c), bf16),
        compiler_params=_params(2),
        name="expert_weights_bf16",
    )(w)


CH = 128
MOE_FC = 1792
GATHER_GROUPS = 3
SCATTER_CHUNKS = 2


def _moe_kernel(nch_ref, klo_ref, khi_ref, clo_ref, chi_ref,
                h_ref, slots_ref, slots_t_ref, gates_t_ref, wg_ref, wu_ref, wd_ref,
                y_ref, xs_scr, ys_scr, gate_scr, *, max_chunks, n_groups, n_f):
    b = pl.program_id(0)
    e = pl.program_id(1)
    f = pl.program_id(2)
    be = b * pl.num_programs(1) + e
    nch = nch_ref[be]

    @pl.when((b == 0) & (e == 0) & (f == 0))
    def _():
        xs_scr[...] = jnp.zeros_like(xs_scr)
        ys_scr[...] = jnp.zeros_like(ys_scr)
        gate_scr[...] = jnp.zeros_like(gate_scr)

    @pl.when((e == 0) & (f == 0))
    def _():
        y_ref[...] = jnp.zeros_like(y_ref)

    @pl.when(f == 0)
    def _gather():
        def windowed(c):
            rows = pl.ds(pl.multiple_of(c * CH, CH), CH)
            rank = (lax.broadcasted_iota(jnp.int32, (CH, SUB), 0) + c * CH).astype(f32)
            lo = klo_ref[be * max_chunks + c]
            hi = khi_ref[be * max_chunks + c]
            g0 = jnp.minimum(lo, n_groups - GATHER_GROUPS)
            hits = [slots_t_ref[e, pl.ds(g0 + j, 1), :] == rank for j in range(GATHER_GROUPS)]
            p = jnp.concatenate([jnp.where(hit, 1.0, 0.0).astype(bf16) for hit in hits], axis=1)
            window = h_ref[pl.ds(pl.multiple_of(g0 * SUB, SUB), GATHER_GROUPS * SUB), :]
            xs_scr[rows, :] = _dot(p, window).astype(bf16)
            gate = jnp.zeros((CH, 1), f32)
            for j, hit in enumerate(hits):
                gate = gate + jnp.sum(jnp.where(hit, gates_t_ref[e, pl.ds(g0 + j, 1), :], 0.0),
                                      axis=1, keepdims=True)
            gate_scr[rows, :] = gate
            return rows, rank, lo, hi, g0

        def spread(rows, rank, lo, hi, g0):
            @pl.when(hi >= g0 + GATHER_GROUPS)
            def _():
                def group(kb, acc):
                    xacc, gacc = acc
                    hit = slots_t_ref[e, pl.ds(kb, 1), :] == rank
                    xacc = xacc + _dot(jnp.where(hit, 1.0, 0.0).astype(bf16),
                                       h_ref[pl.ds(pl.multiple_of(kb * SUB, SUB), SUB), :])
                    gacc = gacc + jnp.sum(jnp.where(hit, gates_t_ref[e, pl.ds(kb, 1), :], 0.0),
                                          axis=1, keepdims=True)
                    return xacc, gacc

                xacc, gacc = lax.fori_loop(lo, hi + 1, group,
                                           (jnp.zeros((CH, D), f32), jnp.zeros((CH, 1), f32)))
                xs_scr[rows, :] = xacc.astype(bf16)
                gate_scr[rows, :] = gacc

        def chunk_pair(i, carry):
            first = windowed(2 * i)
            second = windowed(2 * i + 1)
            spread(*first)
            spread(*second)
            return carry

        lax.fori_loop(0, (nch + 1) // 2, chunk_pair, 0)

    def expert_rows(r0, m, stage):
        rows = pl.ds(r0, m)
        xs = xs_scr[rows, :]
        g = _dot(xs, wg_ref[...])
        u = _dot(xs, wu_ref[...])
        a = g * (1.0 / (1.0 + jnp.exp(-g))) * u * gate_scr[rows, :]
        d = _dot(a.astype(bf16), wd_ref[...])
        if stage == "first":
            ys_scr[rows, :] = d
        elif stage == "middle":
            ys_scr[rows, :] += d
        else:
            xs_scr[rows, :] = (ys_scr[rows, :] + d).astype(bf16)

    def run_experts(stage):
        def pair(i, carry):
            expert_rows(pl.multiple_of(i * 2 * CH, 2 * CH), 2 * CH, stage)
            return carry

        lax.fori_loop(0, nch // 2, pair, 0)

        @pl.when(nch % 2 == 1)
        def _():
            expert_rows(pl.multiple_of((nch - 1) * CH, CH), CH, stage)

    @pl.when(f == 0)
    def _():
        run_experts("first")

    if n_f > 2:
        @pl.when((f > 0) & (f < n_f - 1))
        def _():
            run_experts("middle")

    @pl.when(f == n_f - 1)
    def _scatter():
        run_experts("last")
        lane_e = lax.broadcasted_iota(jnp.int32, (SUB, LANES), 1)
        lane_w = lax.broadcasted_iota(jnp.int32, (SUB, SCATTER_CHUNKS * CH), 1)
        lane_c = lax.broadcasted_iota(jnp.int32, (SUB, CH), 1)

        def expert_col(rows):
            return jnp.sum(jnp.where(lane_e == e, slots_ref[rows, :], 0.0), axis=1, keepdims=True)

        for kb in range(n_groups):
            rows = slice(kb * SUB, (kb + 1) * SUB)
            c0 = jnp.minimum(clo_ref[be * n_groups + kb], max_chunks - SCATTER_CHUNKS)
            pt = jnp.where(expert_col(rows) == (lane_w + c0 * CH).astype(f32), 1.0, 0.0).astype(bf16)
            window = xs_scr[pl.ds(pl.multiple_of(c0 * CH, CH), SCATTER_CHUNKS * CH), :]
            y_ref[rows, :] += _dot(pt, window)

        def spill(kb, carry):
            rows = pl.ds(pl.multiple_of(kb * SUB, SUB), SUB)
            c0 = jnp.minimum(clo_ref[be * n_groups + kb], max_chunks - SCATTER_CHUNKS)

            def chunk(c, carry2):
                pt = jnp.where(expert_col(rows) == (lane_c + c * CH).astype(f32), 1.0, 0.0).astype(bf16)
                y_ref[rows, :] += _dot(pt, xs_scr[pl.ds(pl.multiple_of(c * CH, CH), CH), :])
                return carry2

            lax.fori_loop(c0 + SCATTER_CHUNKS, chi_ref[be * n_groups + kb] + 1, chunk, 0)
            return carry

        lax.fori_loop(0, n_groups, spill, 0)


def _moe(h, slots, slots_t, gates_t, tables, wg, wu, wd):
    B, T, _ = h.shape
    n_e, _, n_hidden = wg.shape
    max_chunks = T // CH
    n_groups = T // SUB
    n_f = n_hidden // MOE_FC
    assert n_f >= 2 and n_f * MOE_FC == n_hidden
    assert max_chunks % 2 == 0
    once = pl.Buffered(1)
    grid_spec = pltpu.PrefetchScalarGridSpec(
        num_scalar_prefetch=len(tables),
        grid=(B, n_e, n_f),
        in_specs=[pl.BlockSpec((None, T, D), lambda b, e, f, *_: (b, 0, 0), pipeline_mode=once),
                  pl.BlockSpec((None, T, LANES), lambda b, e, f, *_: (b, 0, 0), pipeline_mode=once),
                  pl.BlockSpec((None, n_e, n_groups, SUB), lambda b, e, f, *_: (b, 0, 0, 0)),
                  pl.BlockSpec((None, n_e, n_groups, SUB), lambda b, e, f, *_: (b, 0, 0, 0)),
                  pl.BlockSpec((None, D, MOE_FC), lambda b, e, f, *_: (e, 0, f)),
                  pl.BlockSpec((None, D, MOE_FC), lambda b, e, f, *_: (e, 0, f)),
                  pl.BlockSpec((None, MOE_FC, D), lambda b, e, f, *_: (e, f, 0))],
        out_specs=pl.BlockSpec((None, T, D), lambda b, e, f, *_: (b, 0, 0), pipeline_mode=once),
        scratch_shapes=[pltpu.VMEM((T, D), bf16), pltpu.VMEM((T, D), f32),
                        pltpu.VMEM((T, 1), f32)])
    return pl.pallas_call(
        functools.partial(_moe_kernel, max_chunks=max_chunks, n_groups=n_groups, n_f=n_f),
        grid_spec=grid_spec,
        out_shape=jax.ShapeDtypeStruct((B, T, D), f32),
        compiler_params=_params(3),
        name="moe_routed",
    )(*tables, h, slots, slots_t, gates_t, wg, wu, wd)


def _routing_tables(slots, gates, cnt, tm):
    B, T, _ = slots.shape
    nsub = tm // SUB
    n_groups = T // SUB
    max_chunks = T // CH

    def expert_major(a):
        return a[:, :, :N_EXPERTS].reshape(B, n_groups, SUB, N_EXPERTS).transpose(0, 3, 1, 2)

    starts = cnt[:, :, :nsub, :N_EXPERTS].reshape(B, n_groups, N_EXPERTS).astype(jnp.int32)
    starts = starts.transpose(0, 2, 1)
    total = cnt[:, -1, nsub, :N_EXPERTS].astype(jnp.int32)
    nch = (total + CH - 1) // CH
    r0 = jnp.arange(max_chunks, dtype=jnp.int32)[None, None, :] * CH
    r1 = jnp.minimum(total[:, :, None], r0 + CH) - 1
    st = starts[:, :, None, :]
    klo = jnp.sum(st <= r0[..., None], axis=-1) - 1
    khi = jnp.sum(st <= r1[..., None], axis=-1) - 1
    live = r0 < total[:, :, None]
    klo = jnp.where(live, klo, 0)
    khi = jnp.where(live, khi, -1)
    ends = jnp.concatenate([starts[:, :, 1:], total[:, :, None]], axis=-1)
    clo = starts // CH
    chi = jnp.where(ends > starts, (ends - 1) // CH, clo - 1)
    tables = tuple(a.astype(jnp.int32).reshape(-1) for a in (nch, klo, khi, clo, chi))
    return expert_major(slots), expert_major(gates), tables


def _resid_ln_kernel(x_ref, y_ref, mod_ref, g_ref, b_ref, o_ref):
    for s in range(x_ref.shape[0] // SUB):
        rows = slice(s * SUB, (s + 1) * SUB)
        o_ref[rows, :] = _residual_ln(x_ref[rows, :], y_ref[rows, :], mod_ref[s, G_F:G_F + 1, :],
                                      g_ref[...], b_ref[...])


def _resid_ln(X, y, modtab, ln_g, ln_b, tm):
    B, T, _ = X.shape
    row_spec = pl.BlockSpec((None, tm, D), lambda b, t: (b, t, 0))
    vec_spec = pl.BlockSpec((1, D), lambda b, t: (0, 0))
    return pl.pallas_call(
        _resid_ln_kernel,
        grid=(B, T // tm),
        in_specs=[row_spec, row_spec,
                  pl.BlockSpec((None, tm // SUB, N_MOD, D), lambda b, t: (b, t, 0, 0)),
                  vec_spec, vec_spec],
        out_specs=row_spec,
        out_shape=jax.ShapeDtypeStruct((B, T, D), f32),
        compiler_params=_params(2),
        name="resid_ln",
    )(X, y, modtab, ln_g, ln_b)


def _rope_tables(n_ctx, n_lat):
    rows = n_lat // GRID_W
    row = jnp.repeat(jnp.arange(rows, dtype=f32), GRID_W)
    col = jnp.tile(jnp.arange(GRID_W, dtype=f32), rows)
    half = HEAD_DIM // 2
    inv = ROPE_THETA ** (-jnp.arange(0, half, 2, dtype=f32) / half)
    ang_r = row[:, None] * inv[None, :]
    ang_c = col[:, None] * inv[None, :]
    ang = jnp.concatenate([ang_r, ang_r, ang_c, ang_c], axis=-1)
    cos, sin = jnp.cos(ang), jnp.sin(ang)
    quarter = (np.arange(HEAD_DIM) // (HEAD_DIM // 4)) % 2
    sa = jnp.where(quarter == 0, -sin, 0.0)
    sb = jnp.where(quarter == 1, sin, 0.0)
    reps = LANES // HEAD_DIM

    def full(tab, ctx_value):
        tab = jnp.concatenate([jnp.full((n_ctx, HEAD_DIM), ctx_value, f32), tab], axis=0)
        return jnp.tile(tab, (1, reps))

    return full(cos, 1.0), full(sa, 0.0), full(sb, 0.0)


def _a_head_perm():
    group = A_HEADS // A_KV_HEADS
    order = []
    for hp in range(A_KV_HEADS // 2):
        for j in range(group):
            order += [(2 * hp) * group + j, (2 * hp + 1) * group + j]
    return np.asarray(order)


def kernel(x, c, ctx, c_ctx, w_mod, b_mod, ln_g, ln_b, a_w_qkv, a_sink, a_w_o,
           b_w_qkv, b_lam_q1, b_lam_k1, b_lam_q2, b_lam_k2, b_subln_g, b_w_o,
           ff_w_gate, ff_w_up, ff_w_down, moe_w_router, moe_w_gate, moe_w_up, moe_w_down):
    B, L, _ = x.shape
    C = ctx.shape[1]
    X = jnp.concatenate([ctx, x], axis=1)

    cond = jnp.concatenate([c, c_ctx[None, :], jnp.zeros((MOD_ROWS - B - 1, D), f32)], axis=0)
    mods = _modulation(cond, w_mod, b_mod).reshape(DEPTH, MOD_ROWS, N_MOD, D)
    rope_tabs = _rope_tables(C, L)
    tri = jnp.asarray(np.tril(np.ones((SUB, SUB), np.float32)), bf16)
    perm = _a_head_perm()
    perm_cols = (perm[:, None] * HEAD_DIM + np.arange(HEAD_DIM)[None, :]).reshape(-1)
    qscale = HEAD_DIM ** -0.5 * LOG2E
    a_qd = A_HEADS * HEAD_DIM
    a_kd = A_KV_HEADS * HEAD_DIM
    b_qd = 2 * B_HEADS * HEAD_DIM

    for i in range(DEPTH):
        j = i // 2
        with_ctx = i < DEPTH - 1
        m_lat = jnp.broadcast_to(mods[i, :B, None], (B, L // SUB, N_MOD, D))
        m_ctx = jnp.broadcast_to(mods[i, B][None, None], (B, C // SUB, N_MOD, D))
        modtab = jnp.concatenate([m_ctx, m_lat], axis=1)
        q_row0 = 0 if with_ctx else C

        if i % 2 == 0:
            w = a_w_qkv[j]
            wq = w[:, :a_qd][:, perm_cols].astype(bf16)
            wk = w[:, a_qd:a_qd + a_kd].astype(bf16)
            wv = w[:, a_qd + a_kd:].astype(bf16)
            q, k, v = _qkv(X, modtab, rope_tabs, wq, wk, wv, qscale)
            sink = a_sink[j][perm].astype(f32) * LOG2E
            o = _attn_a(q, k, v, sink, C, q_row0)
            w_o = a_w_o[j][perm_cols, :].astype(bf16)
        else:
            w = b_w_qkv[j]
            wq = w[:, :b_qd].astype(bf16)
            wk = w[:, b_qd:2 * b_qd].astype(bf16)
            wv = w[:, 2 * b_qd:].astype(bf16)
            q, k, v = _qkv(X, modtab, rope_tabs, wq, wk, wv, qscale)
            lam_init = 0.8 - 0.6 * math.exp(-0.3 * i)
            lam = (jnp.exp(jnp.sum(b_lam_q1[j] * b_lam_k1[j]))
                   - jnp.exp(jnp.sum(b_lam_q2[j] * b_lam_k2[j])) + lam_init).reshape(1)
            o = _attn_b(q, k, v, lam, b_subln_g[j].reshape(1, B_V_DIM), C, 1.0 - lam_init, q_row0)
            w_o = b_w_o[j].astype(bf16)
        lng0 = ln_g[i, 0].reshape(1, D)
        lnb0 = ln_b[i, 0].reshape(1, D)
        if with_ctx:
            X = _oproj(o, X, modtab, w_o, lng0, lnb0, TM, 0)
            tm = TM
        else:
            X = _oproj(o, X, modtab, w_o, lng0, lnb0, SUB, C)
            modtab = m_lat
            tm = TM_LATENT

        lng = ln_g[i, 1].reshape(1, D)
        lnb = ln_b[i, 1].reshape(1, D)
        if i % 2 == 0:
            X = _ffn(X, modtab, ff_w_gate[j].astype(bf16), ff_w_up[j].astype(bf16),
                     ff_w_down[j].astype(bf16), lng, lnb, tm)
        else:
            h, gates, slots, cnt = _router(X, modtab, moe_w_router[j], tri, tm)
            slots_t, gates_t, tables = _routing_tables(slots, gates, cnt, tm)
            y = _moe(h, slots, slots_t, gates_t, tables, _expert_weights_bf16(moe_w_gate, j),
                     _expert_weights_bf16(moe_w_up, j), _expert_weights_bf16(moe_w_down, j))
            X = _resid_ln(X, y, modtab, lng, lnb, tm)
    return X
```

```python
import functools
import math

import numpy as np
import jax
import jax.numpy as jnp
from jax import lax
from jax.experimental import pallas as pl
from jax.experimental.pallas import tpu as pltpu

f32 = jnp.float32
bf16 = jnp.bfloat16

D = 1024
DEPTH = 4
GRID_W = 64
N_MOD = 6
HEAD_DIM = 64
A_HEADS = 16
A_KV_HEADS = 4
WINDOW = 128
B_HEADS = 8
B_V_DIM = 128
N_EXPERTS = 8
ROPE_THETA = 10000.0
LN_EPS = 1e-5
RMS_EPS = 1e-5
NEG_INF = -1e30
ALPHA = (2.0 * DEPTH) ** 0.25
LOG2E = 1.4426950408889634

LANES = 128
SUB = 256
TM = 768
NSUB = TM // SUB
TM_LATENT = 1024
QB = 128
KB = 128
QB_B = 256
VMEM_LIMIT = 56 * 1024 * 1024

SH_A, SC_A, G_A, SH_F, SC_F, G_F = range(6)


def _dot(a, b):
    return jnp.dot(a, b, preferred_element_type=f32)


def _params(n_grid):
    return pltpu.CompilerParams(dimension_semantics=("arbitrary",) * n_grid,
                                vmem_limit_bytes=VMEM_LIMIT)


MOD_ROWS = 24
MOD_NB = 1536


def _mod_kernel(c_ref, w_ref, b_ref, o_ref):
    cond = c_ref[...]
    s = cond * (1.0 / (1.0 + jnp.exp(-cond)))
    o_ref[...] = _dot(s.astype(bf16), w_ref[...].astype(bf16)) + b_ref[...]


def _modulation(cond, w_mod, b_mod):
    n = N_MOD * D
    return pl.pallas_call(
        _mod_kernel,
        grid=(DEPTH, n // MOD_NB),
        in_specs=[pl.BlockSpec((MOD_ROWS, D), lambda i, j: (0, 0)),
                  pl.BlockSpec((None, D, MOD_NB), lambda i, j: (i, 0, j)),
                  pl.BlockSpec((None, 1, MOD_NB), lambda i, j: (i, 0, j))],
        out_specs=pl.BlockSpec((None, MOD_ROWS, MOD_NB), lambda i, j: (i, 0, j)),
        out_shape=jax.ShapeDtypeStruct((DEPTH, MOD_ROWS, n), f32),
        compiler_params=_params(2),
        name="modulation",
    )(cond, w_mod, b_mod.reshape(DEPTH, 1, n))


def _modulated(x_ref, mod_ref, s, sh, sc):
    x = x_ref[s * SUB:(s + 1) * SUB, :]
    return x * (1.0 + mod_ref[s, sc:sc + 1, :]) + mod_ref[s, sh:sh + 1, :]


def _qkv_kernel(x_ref, mod_ref, cos_ref, sa_ref, sb_ref, wq_ref, wk_ref, wv_ref,
                q_ref, kt_ref, v_ref, *, qscale):
    for s in range(x_ref.shape[0] // SUB):
        rows = slice(s * SUB, (s + 1) * SUB)
        h = _modulated(x_ref, mod_ref, s, SH_A, SC_A).astype(bf16)
        cos = cos_ref[rows, :]
        sa = sa_ref[rows, :]
        sb = sb_ref[rows, :]

        def rope(y):
            return (y * cos + pltpu.roll(y, LANES - 16, 1) * sa + pltpu.roll(y, 16, 1) * sb)

        yq = _dot(h, wq_ref[...])
        for j in range(yq.shape[1] // LANES):
            blk = slice(j * LANES, (j + 1) * LANES)
            q_ref[rows, blk] = (rope(yq[:, blk]) * qscale).astype(bf16)
        yk = _dot(h, wk_ref[...])
        for j in range(yk.shape[1] // LANES):
            kr = rope(yk[:, j * LANES:(j + 1) * LANES])
            for r in range(SUB // KB):
                kt_ref[j, s * (SUB // KB) + r] = kr[r * KB:(r + 1) * KB, :].T.astype(bf16)
        v_ref[rows, :] = _dot(h, wv_ref[...]).astype(bf16)


def _qkv(X, modtab, rope_tabs, wq, wk, wv, qscale):
    B, T, _ = X.shape
    nq, nk, nv = wq.shape[1], wk.shape[1], wv.shape[1]
    cos, sa, sb = rope_tabs
    tab_spec = pl.BlockSpec((TM, LANES), lambda b, t: (t, 0))
    return pl.pallas_call(
        functools.partial(_qkv_kernel, qscale=qscale),
        grid=(B, T // TM),
        in_specs=[pl.BlockSpec((None, TM, D), lambda b, t: (b, t, 0)),
                  pl.BlockSpec((None, NSUB, N_MOD, D), lambda b, t: (b, t, 0, 0)),
                  tab_spec, tab_spec, tab_spec,
                  pl.BlockSpec((D, nq), lambda b, t: (0, 0)),
                  pl.BlockSpec((D, nk), lambda b, t: (0, 0)),
                  pl.BlockSpec((D, nv), lambda b, t: (0, 0))],
        out_specs=[pl.BlockSpec((None, TM, nq), lambda b, t: (b, t, 0)),
                   pl.BlockSpec((None, nk // LANES, TM // KB, LANES, KB), lambda b, t: (b, 0, t, 0, 0)),
                   pl.BlockSpec((None, TM, nv), lambda b, t: (b, t, 0))],
        out_shape=[jax.ShapeDtypeStruct((B, T, nq), bf16),
                   jax.ShapeDtypeStruct((B, nk // LANES, T // KB, LANES, KB), bf16),
                   jax.ShapeDtypeStruct((B, T, nv), bf16)],
        compiler_params=_params(2),
        name="qkv_proj",
    )(X, modtab, cos, sa, sb, wq, wk, wv)


A_PAIRS = 4


def _chain_update(state, q, kt, v, mask):
    s = _dot(q, kt)
    if mask is not None:
        s = jnp.where(mask, s, NEG_INF)
    if state is None:
        m_new = jnp.max(s, axis=1, keepdims=True)
        p = jnp.exp2(s - m_new)
        return m_new, jnp.sum(p, axis=1, keepdims=True), _dot(p.astype(bf16), v)
    m, l, acc = state
    m_new = jnp.maximum(m, jnp.max(s, axis=1, keepdims=True))
    alpha = jnp.exp2(m - m_new)
    p = jnp.exp2(s - m_new)
    return (m_new, alpha * l + jnp.sum(p, axis=1, keepdims=True),
            alpha * acc + _dot(p.astype(bf16), v))


def _half_masked(q):
    lo = lax.broadcasted_iota(jnp.int32, q.shape, 1) < HEAD_DIM
    zero = jnp.zeros_like(q)
    return [jnp.where(lo, q, zero), jnp.where(lo, zero, q)]


def _attn_a_kernel(sink_ref, q_ref, kt_ref, v_ref, o_ref, *, n_ctx, n_lat, q_block0):
    n = pl.program_id(1) + q_block0
    n_ctx_tiles = n_ctx // KB
    n_kv_pairs = kt_ref.shape[0]
    lo = lax.broadcasted_iota(jnp.int32, (QB, LANES), 1) < HEAD_DIM
    q = q_ref[...]
    chains = []
    for p in range(n_kv_pairs * A_PAIRS):
        chains += _half_masked(q[:, p * LANES:(p + 1) * LANES])

    def init(i):
        return (jnp.full((QB, 1), sink_ref[i], f32),
                jnp.ones((QB, 1), f32), jnp.zeros((QB, LANES), f32))

    def finish(states):
        for p in range(n_kv_pairs * A_PAIRS):
            (_, l0, a0), (_, l1, a1) = states[2 * p], states[2 * p + 1]
            o_ref[:, p * LANES:(p + 1) * LANES] = jnp.where(lo, a0 * (1.0 / l0),
                                                          a1 * (1.0 / l1)).astype(bf16)

    def keys(hp, tiles):
        return jnp.concatenate([kt_ref[hp, j] for j in tiles], axis=1)

    def vals(hp, row0, n_rows):
        return v_ref[pl.ds(row0, n_rows), hp * LANES:(hp + 1) * LANES]

    @pl.when(n < n_ctx // QB)
    def _():
        states = []
        for i in range(len(chains)):
            hp = i // (2 * A_PAIRS)
            states.append(_chain_update(init(i), chains[i], keys(hp, range(n_ctx_tiles)),
                                        vals(hp, 0, n_ctx), None))
        finish(states)

    @pl.when(n >= n_ctx // QB)
    def _():
        nl = n - n_ctx // QB
        first = jnp.clip(nl - 1, 0, n_lat // KB - 3)
        jb = n_ctx_tiles + first
        row0 = pl.multiple_of(n_ctx + first * KB, KB)
        rel = (first * KB + lax.broadcasted_iota(jnp.int32, (QB, 3 * KB), 1)
               - nl * QB - lax.broadcasted_iota(jnp.int32, (QB, 3 * KB), 0))
        ok = jnp.abs(rel) <= WINDOW
        states = []
        for i in range(len(chains)):
            hp = i // (2 * A_PAIRS)
            st = _chain_update(init(i), chains[i], keys(hp, range(n_ctx_tiles)),
                               vals(hp, 0, n_ctx), None)
            st = _chain_update(st, chains[i], keys(hp, (jb, jb + 1)),
                               vals(hp, row0, 2 * KB), ok[:, :2 * KB])
            st = _chain_update(st, chains[i], keys(hp, (jb + 2,)),
                               vals(hp, row0 + 2 * KB, KB), ok[:, 2 * KB:])
            states.append(st)
        finish(states)


def _attn_a(q, kt, v, sink, n_ctx, q_row0):
    B, T, nq = q.shape
    n_lat = T - n_ctx
    n_kv_pairs = kt.shape[1]
    q_block0 = q_row0 // QB
    return pl.pallas_call(
        functools.partial(_attn_a_kernel, n_ctx=n_ctx, n_lat=n_lat, q_block0=q_block0),
        grid=(B, T // QB - q_block0),
        in_specs=[pl.BlockSpec(memory_space=pltpu.SMEM),
                  pl.BlockSpec((None, QB, nq), lambda b, n: (b, n + q_block0, 0)),
                  pl.BlockSpec((None, n_kv_pairs, T // KB, LANES, KB), lambda b, n: (b, 0, 0, 0, 0)),
                  pl.BlockSpec((None, T, n_kv_pairs * LANES), lambda b, n: (b, 0, 0))],
        out_specs=pl.BlockSpec((None, QB, nq), lambda b, n: (b, n + q_block0, 0)),
        out_shape=jax.ShapeDtypeStruct((B, T, nq), bf16),
        compiler_params=_params(2),
        name="attn_window",
    )(sink, q, kt, v)


B_CHUNK_TILES = 2
B_STEP_HEADS = 8


def _attn_b_kernel(lam_ref, q_ref, kt_ref, v_ref, g_ref, o_ref, *, n_ctx, out_scale, q_block0):
    n = pl.program_id(2) + q_block0
    lam = lam_ref[0]
    n_tiles = kt_ref.shape[1]
    q = q_ref[...]
    units = [(h, r) for h in range(B_STEP_HEADS) for r in range(QB_B // QB)]
    chains = []
    for h, r in units:
        chains += _half_masked(q[r * QB:(r + 1) * QB, h * LANES:(h + 1) * LANES])

    def run(n_key_tiles):
        states = [None] * len(chains)
        for t0 in range(0, n_key_tiles, B_CHUNK_TILES):
            t1 = min(t0 + B_CHUNK_TILES, n_key_tiles)
            for u, (h, r) in enumerate(units):
                kt = jnp.concatenate([kt_ref[h, j] for j in range(t0, t1)], axis=1)
                v = v_ref[t0 * KB:t1 * KB, h * LANES:(h + 1) * LANES]
                for i in (2 * u, 2 * u + 1):
                    states[i] = _chain_update(states[i], chains[i], kt, v, None)
        for u, (h, r) in enumerate(units):
            (_, l1, a1), (_, l2, a2) = states[2 * u], states[2 * u + 1]
            od = a1 * (1.0 / l1) - lam * (a2 * (1.0 / l2))
            y = od * lax.rsqrt(jnp.mean(od * od, axis=1, keepdims=True) + RMS_EPS)
            o_ref[r * QB:(r + 1) * QB, h * LANES:(h + 1) * LANES] = (
                y * g_ref[...] * out_scale).astype(bf16)

    @pl.when(n < n_ctx // QB_B)
    def _():
        run(n_ctx // KB)

    @pl.when(n >= n_ctx // QB_B)
    def _():
        run(n_tiles)


def _attn_b(q, kt, v, lam, subln_g, n_ctx, out_scale, q_row0):
    B, T, nq = q.shape
    cols = B_STEP_HEADS * LANES
    q_block0 = q_row0 // QB_B
    return pl.pallas_call(
        functools.partial(_attn_b_kernel, n_ctx=n_ctx, out_scale=out_scale, q_block0=q_block0),
        grid=(B, nq // cols, T // QB_B - q_block0),
        in_specs=[pl.BlockSpec(memory_space=pltpu.SMEM),
                  pl.BlockSpec((None, QB_B, cols), lambda b, h, n: (b, n + q_block0, h)),
                  pl.BlockSpec((None, B_STEP_HEADS, T // KB, LANES, KB), lambda b, h, n: (b, h, 0, 0, 0)),
                  pl.BlockSpec((None, T, cols), lambda b, h, n: (b, 0, h)),
                  pl.BlockSpec((1, LANES), lambda b, h, n: (0, 0))],
        out_specs=pl.BlockSpec((None, QB_B, cols), lambda b, h, n: (b, n + q_block0, h)),
        out_shape=jax.ShapeDtypeStruct((B, T, nq), bf16),
        compiler_params=_params(3),
        name="attn_diff",
    )(lam, q, kt, v, subln_g)


def _residual_ln(x, y, gate, ln_g, ln_b):
    z = ALPHA * x + (1.0 + gate) * y
    mu = jnp.mean(z, axis=1, keepdims=True)
    zc = z - mu
    var = jnp.mean(zc * zc, axis=1, keepdims=True)
    return zc * lax.rsqrt(var + LN_EPS) * ln_g + ln_b


def _oproj_kernel(o_ref, x_ref, mod_ref, w_ref, g_ref, b_ref, xo_ref):
    for s in range(x_ref.shape[0] // SUB):
        rows = slice(s * SUB, (s + 1) * SUB)
        y = _dot(o_ref[rows, :], w_ref[...])
        xo_ref[rows, :] = _residual_ln(x_ref[rows, :], y, mod_ref[s, G_A:G_A + 1, :],
                                       g_ref[...], b_ref[...])


def _oproj(o, X, modtab, w_o, ln_g, ln_b, tm, row0):
    B, T, _ = X.shape
    blk0 = row0 // tm
    in_rows = pl.BlockSpec((None, tm, D), lambda b, t: (b, t + blk0, 0))
    vec_spec = pl.BlockSpec((1, D), lambda b, t: (0, 0))
    return pl.pallas_call(
        _oproj_kernel,
        grid=(B, (T - row0) // tm),
        in_specs=[in_rows, in_rows,
                  pl.BlockSpec((None, tm // SUB, N_MOD, D), lambda b, t: (b, t + blk0, 0, 0)),
                  pl.BlockSpec((D, D), lambda b, t: (0, 0)),
                  vec_spec, vec_spec],
        out_specs=pl.BlockSpec((None, tm, D), lambda b, t: (b, t, 0)),
        out_shape=jax.ShapeDtypeStruct((B, T - row0, D), f32),
        compiler_params=_params(2),
        name="oproj_ln",
    )(o, X, modtab, w_o, ln_g, ln_b)


FFN_FC = 2816


def _ffn_kernel(x_ref, mod_ref, wg_ref, wu_ref, wd_ref, lng_ref, lnb_ref, o_ref):
    n_hidden = wg_ref.shape[1]
    for s in range(x_ref.shape[0] // SUB):
        rows = slice(s * SUB, (s + 1) * SUB)
        x = x_ref[rows, :]
        h = (x * (1.0 + mod_ref[s, SC_F:SC_F + 1, :]) + mod_ref[s, SH_F:SH_F + 1, :]).astype(bf16)
        y = None
        for c0 in range(0, n_hidden, FFN_FC):
            cols = slice(c0, c0 + FFN_FC)
            g = _dot(h, wg_ref[:, cols])
            u = _dot(h, wu_ref[:, cols])
            a = g * (1.0 / (1.0 + jnp.exp(-g))) * u
            d = _dot(a.astype(bf16), wd_ref[cols, :])
            y = d if y is None else y + d
        o_ref[rows, :] = _residual_ln(x, y, mod_ref[s, G_F:G_F + 1, :], lng_ref[...], lnb_ref[...])


def _ffn(X, modtab, wg, wu, wd, ln_g, ln_b, tm):
    B, T, _ = X.shape
    n_hidden = wg.shape[1]
    row_spec = pl.BlockSpec((None, tm, D), lambda b, t: (b, t, 0))
    vec_spec = pl.BlockSpec((1, D), lambda b, t: (0, 0))
    once = pl.Buffered(1)
    return pl.pallas_call(
        _ffn_kernel,
        grid=(B, T // tm),
        in_specs=[row_spec, pl.BlockSpec((None, tm // SUB, N_MOD, D), lambda b, t: (b, t, 0, 0)),
                  pl.BlockSpec((D, n_hidden), lambda b, t: (0, 0), pipeline_mode=once),
                  pl.BlockSpec((D, n_hidden), lambda b, t: (0, 0), pipeline_mode=once),
                  pl.BlockSpec((n_hidden, D), lambda b, t: (0, 0), pipeline_mode=once),
                  vec_spec, vec_spec],
        out_specs=row_spec,
        out_shape=jax.ShapeDtypeStruct((B, T, D), f32),
        compiler_params=_params(2),
        name="ffn_dense",
    )(X, modtab, wg, wu, wd, ln_g, ln_b)


CNT_ROWS = 8


def _router_kernel(x_ref, mod_ref, wr_hi_ref, wr_lo_ref, tri_ref,
                   h_ref, gates_ref, slots_ref, cnt_ref, carry_scr):
    t = pl.program_id(1)

    @pl.when(t == 0)
    def _():
        carry_scr[...] = jnp.zeros_like(carry_scr)

    lane = lax.broadcasted_iota(jnp.int32, (SUB, LANES), 1)
    nsub = x_ref.shape[0] // SUB
    starts = []
    for s in range(nsub):
        rows = slice(s * SUB, (s + 1) * SUB)
        h = _modulated(x_ref, mod_ref, s, SH_F, SC_F)
        h_hi = h.astype(bf16)
        h_ref[rows, :] = h_hi
        h_lo = (h - h_hi.astype(f32)).astype(bf16)
        logits = (_dot(h_hi, wr_hi_ref[...]) + _dot(h_lo, wr_hi_ref[...])
                  + _dot(h_hi, wr_lo_ref[...]))
        lg = jnp.where(lane < N_EXPERTS, logits, -jnp.inf)
        v1 = jnp.max(lg, axis=1, keepdims=True)
        i1 = jnp.min(jnp.where(lg == v1, lane, LANES), axis=1, keepdims=True)
        lg2 = jnp.where(lane == i1, -jnp.inf, lg)
        v2 = jnp.max(lg2, axis=1, keepdims=True)
        i2 = jnp.min(jnp.where(lg2 == v2, lane, LANES), axis=1, keepdims=True)
        ex = jnp.exp(v2 - v1)
        w1 = 1.0 / (1.0 + ex)
        w2 = ex / (1.0 + ex)
        sel = (lane == i1) | (lane == i2)
        gates_ref[rows, :] = jnp.where(lane == i1, w1, jnp.where(lane == i2, w2, 0.0))
        cum = _dot(tri_ref[...], jnp.where(sel, 1.0, 0.0).astype(bf16))
        carry = carry_scr[...]
        starts.append(carry)
        slots_ref[rows, :] = jnp.where(sel, carry + cum - 1.0, -1.0)
        carry_scr[...] = carry + cum[SUB - 1:SUB, :]
    pad = jnp.zeros((CNT_ROWS - nsub - 1, LANES), f32)
    cnt_ref[...] = jnp.concatenate(starts + [carry_scr[...], pad], axis=0)


def _router(X, modtab, w_router, tri, tm):
    B, T, _ = X.shape
    nt = T // tm
    assert tm // SUB < CNT_ROWS
    w = jnp.pad(w_router, ((0, 0), (0, LANES - N_EXPERTS)))
    w_hi = w.astype(bf16)
    w_lo = (w - w_hi.astype(f32)).astype(bf16)
    row128 = pl.BlockSpec((None, tm, LANES), lambda b, t: (b, t, 0))
    w_spec = pl.BlockSpec((D, LANES), lambda b, t: (0, 0))
    return pl.pallas_call(
        _router_kernel,
        grid=(B, nt),
        in_specs=[pl.BlockSpec((None, tm, D), lambda b, t: (b, t, 0)),
                  pl.BlockSpec((None, tm // SUB, N_MOD, D), lambda b, t: (b, t, 0, 0)),
                  w_spec, w_spec,
                  pl.BlockSpec((SUB, SUB), lambda b, t: (0, 0))],
        out_specs=[pl.BlockSpec((None, tm, D), lambda b, t: (b, t, 0)), row128, row128,
                   pl.BlockSpec((None, None, CNT_ROWS, LANES), lambda b, t: (b, t, 0, 0))],
        out_shape=[jax.ShapeDtypeStruct((B, T, D), bf16),
                   jax.ShapeDtypeStruct((B, T, LANES), f32),
                   jax.ShapeDtypeStruct((B, T, LANES), f32),
                   jax.ShapeDtypeStruct((B, nt, CNT_ROWS, LANES), f32)],
        scratch_shapes=[pltpu.VMEM((1, LANES), f32)],
        compiler_params=_params(2),
        name="router",
    )(X, modtab, w_hi, w_lo, tri)


CAST_BLOCK_BYTES = 4 * 1024 * 1024


def _cast_kernel(w_ref, o_ref):
    o_ref[...] = w_ref[...].astype(bf16)


def _expert_weights_bf16(w, layer):
    _, n_e, n_r, n_c = w.shape
    rows = 8
    while n_r % (2 * rows) == 0 and 2 * rows * n_c * 4 <= CAST_BLOCK_BYTES:
        rows *= 2
    return pl.pallas_call(
        _cast_kernel,
        grid=(n_e, n_r // rows),
        in_specs=[pl.BlockSpec((None, None, rows, n_c), lambda e, r: (layer, e, r, 0))],
        out_specs=pl.BlockSpec((None, rows, n_c), lambda e, r: (e, r, 0)),
        out_shape=jax.ShapeDtypeStruct((n_e, n_r, n_c), bf16),
        compiler_params=_params(2),
        name="expert_weights_bf16",
    )(w)


CH = 128
MOE_FC = 1792
GATHER_GROUPS = 3
SCATTER_CHUNKS = 2


def _moe_kernel(nch_ref, klo_ref, khi_ref, clo_ref, chi_ref,
                h_ref, slots_ref, slots_t_ref, gates_t_ref, wg_ref, wu_ref, wd_ref,
                y_ref, xs_scr, ys_scr, gate_scr, *, max_chunks, n_groups, n_f):
    b = pl.program_id(0)
    e = pl.program_id(1)
    f = pl.program_id(2)
    be = b * pl.num_programs(1) + e
    nch = nch_ref[be]

    @pl.when((b == 0) & (e == 0) & (f == 0))
    def _():
        xs_scr[...] = jnp.zeros_like(xs_scr)
        ys_scr[...] = jnp.zeros_like(ys_scr)
        gate_scr[...] = jnp.zeros_like(gate_scr)

    @pl.when((e == 0) & (f == 0))
    def _():
        y_ref[...] = jnp.zeros_like(y_ref)

    @pl.when(f == 0)
    def _gather():
        def windowed(c):
            rows = pl.ds(pl.multiple_of(c * CH, CH), CH)
            rank = (lax.broadcasted_iota(jnp.int32, (CH, SUB), 0) + c * CH).astype(f32)
            lo = klo_ref[be * max_chunks + c]
            hi = khi_ref[be * max_chunks + c]
            g0 = jnp.minimum(lo, n_groups - GATHER_GROUPS)
            hits = [slots_t_ref[e, pl.ds(g0 + j, 1), :] == rank for j in range(GATHER_GROUPS)]
            p = jnp.concatenate([jnp.where(hit, 1.0, 0.0).astype(bf16) for hit in hits], axis=1)
            window = h_ref[pl.ds(pl.multiple_of(g0 * SUB, SUB), GATHER_GROUPS * SUB), :]
            xs_scr[rows, :] = _dot(p, window).astype(bf16)
            gate = jnp.zeros((CH, 1), f32)
            for j, hit in enumerate(hits):
                gate = gate + jnp.sum(jnp.where(hit, gates_t_ref[e, pl.ds(g0 + j, 1), :], 0.0),
                                      axis=1, keepdims=True)
            gate_scr[rows, :] = gate
            return rows, rank, lo, hi, g0

        def spread(rows, rank, lo, hi, g0):
            @pl.when(hi >= g0 + GATHER_GROUPS)
            def _():
                def group(kb, acc):
                    xacc, gacc = acc
                    hit = slots_t_ref[e, pl.ds(kb, 1), :] == rank
                    xacc = xacc + _dot(jnp.where(hit, 1.0, 0.0).astype(bf16),
                                       h_ref[pl.ds(pl.multiple_of(kb * SUB, SUB), SUB), :])
                    gacc = gacc + jnp.sum(jnp.where(hit, gates_t_ref[e, pl.ds(kb, 1), :], 0.0),
                                          axis=1, keepdims=True)
                    return xacc, gacc

                xacc, gacc = lax.fori_loop(lo, hi + 1, group,
                                           (jnp.zeros((CH, D), f32), jnp.zeros((CH, 1), f32)))
                xs_scr[rows, :] = xacc.astype(bf16)
                gate_scr[rows, :] = gacc

        def chunk_pair(i, carry):
            first = windowed(2 * i)
            second = windowed(2 * i + 1)
            spread(*first)
            spread(*second)
            return carry

        lax.fori_loop(0, (nch + 1) // 2, chunk_pair, 0)

    def expert_rows(r0, m, stage):
        rows = pl.ds(r0, m)
        xs = xs_scr[rows, :]
        g = _dot(xs, wg_ref[...])
        u = _dot(xs, wu_ref[...])
        a = g * (1.0 / (1.0 + jnp.exp(-g))) * u * gate_scr[rows, :]
        d = _dot(a.astype(bf16), wd_ref[...])
        if stage == "first":
            ys_scr[rows, :] = d
        elif stage == "middle":
            ys_scr[rows, :] += d
        else:
            xs_scr[rows, :] = (ys_scr[rows, :] + d).astype(bf16)

    def run_experts(stage):
        def pair(i, carry):
            expert_rows(pl.multiple_of(i * 2 * CH, 2 * CH), 2 * CH, stage)
            return carry

        lax.fori_loop(0, nch // 2, pair, 0)

        @pl.when(nch % 2 == 1)
        def _():
            expert_rows(pl.multiple_of((nch - 1) * CH, CH), CH, stage)

    @pl.when(f == 0)
    def _():
        run_experts("first")

    if n_f > 2:
        @pl.when((f > 0) & (f < n_f - 1))
        def _():
            run_experts("middle")

    @pl.when(f == n_f - 1)
    def _scatter():
        run_experts("last")
        lane_e = lax.broadcasted_iota(jnp.int32, (SUB, LANES), 1)
        lane_w = lax.broadcasted_iota(jnp.int32, (SUB, SCATTER_CHUNKS * CH), 1)
        lane_c = lax.broadcasted_iota(jnp.int32, (SUB, CH), 1)

        def expert_col(rows):
            return jnp.sum(jnp.where(lane_e == e, slots_ref[rows, :], 0.0), axis=1, keepdims=True)

        for kb in range(n_groups):
            rows = slice(kb * SUB, (kb + 1) * SUB)
            c0 = jnp.minimum(clo_ref[be * n_groups + kb], max_chunks - SCATTER_CHUNKS)
            pt = jnp.where(expert_col(rows) == (lane_w + c0 * CH).astype(f32), 1.0, 0.0).astype(bf16)
            window = xs_scr[pl.ds(pl.multiple_of(c0 * CH, CH), SCATTER_CHUNKS * CH), :]
            y_ref[rows, :] += _dot(pt, window)

        def spill(kb, carry):
            rows = pl.ds(pl.multiple_of(kb * SUB, SUB), SUB)
            c0 = jnp.minimum(clo_ref[be * n_groups + kb], max_chunks - SCATTER_CHUNKS)

            def chunk(c, carry2):
                pt = jnp.where(expert_col(rows) == (lane_c + c * CH).astype(f32), 1.0, 0.0).astype(bf16)
                y_ref[rows, :] += _dot(pt, xs_scr[pl.ds(pl.multiple_of(c * CH, CH), CH), :])
                return carry2

            lax.fori_loop(c0 + SCATTER_CHUNKS, chi_ref[be * n_groups + kb] + 1, chunk, 0)
            return carry

        lax.fori_loop(0, n_groups, spill, 0)


def _moe(h, slots, slots_t, gates_t, tables, wg, wu, wd):
    B, T, _ = h.shape
    n_e, _, n_hidden = wg.shape
    max_chunks = T // CH
    n_groups = T // SUB
    n_f = n_hidden // MOE_FC
    assert n_f >= 2 and n_f * MOE_FC == n_hidden
    assert max_chunks % 2 == 0
    once = pl.Buffered(1)
    grid_spec = pltpu.PrefetchScalarGridSpec(
        num_scalar_prefetch=len(tables),
        grid=(B, n_e, n_f),
        in_specs=[pl.BlockSpec((None, T, D), lambda b, e, f, *_: (b, 0, 0), pipeline_mode=once),
                  pl.BlockSpec((None, T, LANES), lambda b, e, f, *_: (b, 0, 0), pipeline_mode=once),
                  pl.BlockSpec((None, n_e, n_groups, SUB), lambda b, e, f, *_: (b, 0, 0, 0)),
                  pl.BlockSpec((None, n_e, n_groups, SUB), lambda b, e, f, *_: (b, 0, 0, 0)),
                  pl.BlockSpec((None, D, MOE_FC), lambda b, e, f, *_: (e, 0, f)),
                  pl.BlockSpec((None, D, MOE_FC), lambda b, e, f, *_: (e, 0, f)),
                  pl.BlockSpec((None, MOE_FC, D), lambda b, e, f, *_: (e, f, 0))],
        out_specs=pl.BlockSpec((None, T, D), lambda b, e, f, *_: (b, 0, 0), pipeline_mode=once),
        scratch_shapes=[pltpu.VMEM((T, D), bf16), pltpu.VMEM((T, D), f32),
                        pltpu.VMEM((T, 1), f32)])
    return pl.pallas_call(
        functools.partial(_moe_kernel, max_chunks=max_chunks, n_groups=n_groups, n_f=n_f),
        grid_spec=grid_spec,
        out_shape=jax.ShapeDtypeStruct((B, T, D), f32),
        compiler_params=_params(3),
        name="moe_routed",
    )(*tables, h, slots, slots_t, gates_t, wg, wu, wd)


def _routing_tables(slots, gates, cnt, tm):
    B, T, _ = slots.shape
    nsub = tm // SUB
    n_groups = T // SUB
    max_chunks = T // CH

    def expert_major(a):
        return a[:, :, :N_EXPERTS].reshape(B, n_groups, SUB, N_EXPERTS).transpose(0, 3, 1, 2)

    starts = cnt[:, :, :nsub, :N_EXPERTS].reshape(B, n_groups, N_EXPERTS).astype(jnp.int32)
    starts = starts.transpose(0, 2, 1)
    total = cnt[:, -1, nsub, :N_EXPERTS].astype(jnp.int32)
    nch = (total + CH - 1) // CH
    r0 = jnp.arange(max_chunks, dtype=jnp.int32)[None, None, :] * CH
    r1 = jnp.minimum(total[:, :, None], r0 + CH) - 1
    st = starts[:, :, None, :]
    klo = jnp.sum(st <= r0[..., None], axis=-1) - 1
    khi = jnp.sum(st <= r1[..., None], axis=-1) - 1
    live = r0 < total[:, :, None]
    klo = jnp.where(live, klo, 0)
    khi = jnp.where(live, khi, -1)
    ends = jnp.concatenate([starts[:, :, 1:], total[:, :, None]], axis=-1)
    clo = starts // CH
    chi = jnp.where(ends > starts, (ends - 1) // CH, clo - 1)
    tables = tuple(a.astype(jnp.int32).reshape(-1) for a in (nch, klo, khi, clo, chi))
    return expert_major(slots), expert_major(gates), tables


def _resid_ln_kernel(x_ref, y_ref, mod_ref, g_ref, b_ref, o_ref):
    for s in range(x_ref.shape[0] // SUB):
        rows = slice(s * SUB, (s + 1) * SUB)
        o_ref[rows, :] = _residual_ln(x_ref[rows, :], y_ref[rows, :], mod_ref[s, G_F:G_F + 1, :],
                                      g_ref[...], b_ref[...])


def _resid_ln(X, y, modtab, ln_g, ln_b, tm):
    B, T, _ = X.shape
    row_spec = pl.BlockSpec((None, tm, D), lambda b, t: (b, t, 0))
    vec_spec = pl.BlockSpec((1, D), lambda b, t: (0, 0))
    return pl.pallas_call(
        _resid_ln_kernel,
        grid=(B, T // tm),
        in_specs=[row_spec, row_spec,
                  pl.BlockSpec((None, tm // SUB, N_MOD, D), lambda b, t: (b, t, 0, 0)),
                  vec_spec, vec_spec],
        out_specs=row_spec,
        out_shape=jax.ShapeDtypeStruct((B, T, D), f32),
        compiler_params=_params(2),
        name="resid_ln",
    )(X, y, modtab, ln_g, ln_b)


def _rope_tables(n_ctx, n_lat):
    rows = n_lat // GRID_W
    row = jnp.repeat(jnp.arange(rows, dtype=f32), GRID_W)
    col = jnp.tile(jnp.arange(GRID_W, dtype=f32), rows)
    half = HEAD_DIM // 2
    inv = ROPE_THETA ** (-jnp.arange(0, half, 2, dtype=f32) / half)
    ang_r = row[:, None] * inv[None, :]
    ang_c = col[:, None] * inv[None, :]
    ang = jnp.concatenate([ang_r, ang_r, ang_c, ang_c], axis=-1)
    cos, sin = jnp.cos(ang), jnp.sin(ang)
    quarter = (np.arange(HEAD_DIM) // (HEAD_DIM // 4)) % 2
    sa = jnp.where(quarter == 0, -sin, 0.0)
    sb = jnp.where(quarter == 1, sin, 0.0)
    reps = LANES // HEAD_DIM

    def full(tab, ctx_value):
        tab = jnp.concatenate([jnp.full((n_ctx, HEAD_DIM), ctx_value, f32), tab], axis=0)
        return jnp.tile(tab, (1, reps))

    return full(cos, 1.0), full(sa, 0.0), full(sb, 0.0)


def _a_head_perm():
    group = A_HEADS // A_KV_HEADS
    order = []
    for hp in range(A_KV_HEADS // 2):
        for j in range(group):
            order += [(2 * hp) * group + j, (2 * hp + 1) * group + j]
    return np.asarray(order)


def kernel(x, c, ctx, c_ctx, w_mod, b_mod, ln_g, ln_b, a_w_qkv, a_sink, a_w_o,
           b_w_qkv, b_lam_q1, b_lam_k1, b_lam_q2, b_lam_k2, b_subln_g, b_w_o,
           ff_w_gate, ff_w_up, ff_w_down, moe_w_router, moe_w_gate, moe_w_up, moe_w_down):
    B, L, _ = x.shape
    C = ctx.shape[1]
    X = jnp.concatenate([ctx, x], axis=1)

    cond = jnp.concatenate([c, c_ctx[None, :], jnp.zeros((MOD_ROWS - B - 1, D), f32)], axis=0)
    mods = _modulation(cond, w_mod, b_mod).reshape(DEPTH, MOD_ROWS, N_MOD, D)
    rope_tabs = _rope_tables(C, L)
    tri = jnp.asarray(np.tril(np.ones((SUB, SUB), np.float32)), bf16)
    perm = _a_head_perm()
    perm_cols = (perm[:, None] * HEAD_DIM + np.arange(HEAD_DIM)[None, :]).reshape(-1)
    qscale = HEAD_DIM ** -0.5 * LOG2E
    a_qd = A_HEADS * HEAD_DIM
    a_kd = A_KV_HEADS * HEAD_DIM
    b_qd = 2 * B_HEADS * HEAD_DIM

    for i in range(DEPTH):
        j = i // 2
        with_ctx = i < DEPTH - 1
        m_lat = jnp.broadcast_to(mods[i, :B, None], (B, L // SUB, N_MOD, D))
        m_ctx = jnp.broadcast_to(mods[i, B][None, None], (B, C // SUB, N_MOD, D))
        modtab = jnp.concatenate([m_ctx, m_lat], axis=1)
        q_row0 = 0 if with_ctx else C

        if i % 2 == 0:
            w = a_w_qkv[j]
            wq = w[:, :a_qd][:, perm_cols].astype(bf16)
            wk = w[:, a_qd:a_qd + a_kd].astype(bf16)
            wv = w[:, a_qd + a_kd:].astype(bf16)
            q, k, v = _qkv(X, modtab, rope_tabs, wq, wk, wv, qscale)
            sink = a_sink[j][perm].astype(f32) * LOG2E
            o = _attn_a(q, k, v, sink, C, q_row0)
            w_o = a_w_o[j][perm_cols, :].astype(bf16)
        else:
            w = b_w_qkv[j]
            wq = w[:, :b_qd].astype(bf16)
            wk = w[:, b_qd:2 * b_qd].astype(bf16)
            wv = w[:, 2 * b_qd:].astype(bf16)
            q, k, v = _qkv(X, modtab, rope_tabs, wq, wk, wv, qscale)
            lam_init = 0.8 - 0.6 * math.exp(-0.3 * i)
            lam = (jnp.exp(jnp.sum(b_lam_q1[j] * b_lam_k1[j]))
                   - jnp.exp(jnp.sum(b_lam_q2[j] * b_lam_k2[j])) + lam_init).reshape(1)
            o = _attn_b(q, k, v, lam, b_subln_g[j].reshape(1, B_V_DIM), C, 1.0 - lam_init, q_row0)
            w_o = b_w_o[j].astype(bf16)
        lng0 = ln_g[i, 0].reshape(1, D)
        lnb0 = ln_b[i, 0].reshape(1, D)
        if with_ctx:
            X = _oproj(o, X, modtab, w_o, lng0, lnb0, TM, 0)
            tm = TM
        else:
            X = _oproj(o, X, modtab, w_o, lng0, lnb0, SUB, C)
            modtab = m_lat
            tm = TM_LATENT

        lng = ln_g[i, 1].reshape(1, D)
        lnb = ln_b[i, 1].reshape(1, D)
        if i % 2 == 0:
            X = _ffn(X, modtab, ff_w_gate[j].astype(bf16), ff_w_up[j].astype(bf16),
                     ff_w_down[j].astype(bf16), lng, lnb, tm)
        else:
            h, gates, slots, cnt = _router(X, modtab, moe_w_router[j], tri, tm)
            slots_t, gates_t, tables = _routing_tables(slots, gates, cnt, tm)
            y = _moe(h, slots, slots_t, gates_t, tables, _expert_weights_bf16(moe_w_gate, j),
                     _expert_weights_bf16(moe_w_up, j), _expert_weights_bf16(moe_w_down, j))
            X = _resid_ln(X, y, modtab, lng, lnb, tm)
    return X
```
